```python
import jax
import jax.numpy as jnp
from jax import lax
import numpy as np

D_MODEL = 1024
BATCH = 8
SEQ = 2048
DEPTH = 4
DEC_BATCH = 128
DEC_SEQ = 8
PAST_LEN = 16384
PAGE_SIZE = 128

N_MIXERS = 4
EPS = 1e-6
CHUNK = 128
D_A = D_MODEL
A_GROUPS = 8
A_GDIM = D_A // A_GROUPS
POOL_WINDOWS = (2, 4, 8, 16)
POOL_GROUPS = len(POOL_WINDOWS)
POOL_GDIM = D_MODEL // POOL_GROUPS
POOL_BUF = max(POOL_WINDOWS) - 1
D_C = D_MODEL
CONV_WIDTH = 31
CONV_BUF = CONV_WIDTH - 1
GDN_HEADS = 8
GDN_DK = 128
GDN_DV = 128
GDN_CONV = 4
GDN_BUF = GDN_CONV - 1
GDN_QKV = GDN_HEADS * (2 * GDN_DK + GDN_DV)
GDN_CHUNK = 64
D_FF = 2816
FFN_CONV = 3
FFN_BUF = FFN_CONV - 1
N_A = (DEPTH + 3) // 4
N_B = (DEPTH + 2) // 4
N_C = (DEPTH + 1) // 4
N_D = DEPTH // 4

kernel_name = 'hybrid_chunkmlp_pool_conformer_gdn_step'

F32 = jnp.float32


def rmsnorm(x, g):
    xf = x.astype(F32)
    y = xf * lax.rsqrt(jnp.mean(xf * xf, axis=-1, keepdims=True) + EPS)
    return y.astype(x.dtype) * g


def layernorm(x, g, b):
    xf = x.astype(F32)
    mu = jnp.mean(xf, axis=-1, keepdims=True)
    var = jnp.mean(jnp.square(xf - mu), axis=-1, keepdims=True)
    return ((xf - mu) * lax.rsqrt(var + EPS)).astype(x.dtype) * g + b


def l2norm(x):
    xf = x.astype(F32)
    return (xf * lax.rsqrt(jnp.sum(xf * xf, axis=-1, keepdims=True) + EPS)).astype(x.dtype)


def causal_dwconv(x, buf, w):
    k, c = w.shape
    t = x.shape[1]
    xp = jnp.concatenate([buf.astype(x.dtype), x], axis=1)
    y = lax.conv_general_dilated(xp, w[:, None, :].astype(x.dtype), window_strides=(1,),
                                 padding='VALID', dimension_numbers=('NWC', 'WIO', 'NWC'),
                                 feature_group_count=c)
    return y, xp[:, t:]


def chunk_mlp(h, start_pos, w_in, b_in, ln_g, ln_b, w_s, b_s, w_out):
    bsz, t, _ = h.shape
    u, v = jnp.split(jax.nn.gelu(h @ w_in + b_in), 2, axis=-1)
    v = layernorm(v, ln_g, ln_b)
    n_chunks = -(-t // CHUNK)
    vp = jnp.pad(v, ((0, 0), (0, n_chunks * CHUNK - t), (0, 0)))
    vp = vp.reshape(bsz, n_chunks, CHUNK, A_GROUPS, A_GDIM)
    causal = jnp.tril(jnp.ones((CHUNK, CHUNK), dtype=bool))
    ws = jnp.where(causal, w_s, 0.0)
    mixed = jnp.einsum('gij,bnjgc->bnigc', ws, vp) + b_s.T[None, None, :, :, None]
    mixed = mixed.reshape(bsz, n_chunks * CHUNK, D_A)[:, :t]
    last_start = ((start_pos + t - 1) // CHUNK) * CHUNK - start_pos
    return (u * mixed) @ w_out, v[:, last_start:]


def multiscale_pool(h, buf, start_pos, w_grp, scale):
    bsz, t, _ = h.shape
    hp = jnp.concatenate([buf.astype(h.dtype), h], axis=1)
    cs = jnp.pad(jnp.cumsum(hp.astype(F32), axis=1), ((0, 0), (1, 0), (0, 0)))
    pos = start_pos + jnp.arange(t)
    outs = []
    for gi, win in enumerate(POOL_WINDOWS):
        sl = slice(gi * POOL_GDIM, (gi + 1) * POOL_GDIM)
        hi = cs[:, POOL_BUF + 1:POOL_BUF + 1 + t, sl]
        lo = cs[:, POOL_BUF + 1 - win:POOL_BUF + 1 - win + t, sl]
        cnt = jnp.minimum(pos + 1, win).astype(F32)[None, :, None]
        outs.append((hi - lo) / cnt)
    pooled = jnp.concatenate(outs, axis=-1).astype(h.dtype) - h
    pooled = pooled.reshape(bsz, t, POOL_GROUPS, POOL_GDIM)
    y = jnp.einsum('btgc,gcd->btgd', pooled, w_grp).reshape(bsz, t, D_MODEL) * scale
    return y, hp[:, t:]


def conformer_conv(h, buf, w_pw1, b_pw1, w_dw, b_dw, ln_g, ln_b, w_pw2, b_pw2):
    a, gt = jnp.split(h @ w_pw1 + b_pw1, 2, axis=-1)
    z = a * jax.nn.sigmoid(gt)
    z, new_buf = causal_dwconv(z, buf, w_dw)
    z = jax.nn.silu(layernorm(z + b_dw, ln_g, ln_b))
    return z @ w_pw2 + b_pw2, new_buf


def chunk_gated_delta(q, k, v, g, beta, s0):
    bsz, t, nh, _ = q.shape
    dv = v.shape[-1]
    c = GDN_CHUNK if t >= GDN_CHUNK else t
    n = -(-t // c)
    pad = n * c - t

    def blocks(a):
        a = jnp.pad(a.astype(F32), ((0, 0), (0, pad)) + ((0, 0),) * (a.ndim - 2))
        a = a.reshape((bsz, n, c) + a.shape[2:])
        return jnp.moveaxis(a, (1, 3), (0, 2))

    qb, kb, vb, gb, bb = blocks(q), blocks(k), blocks(v), blocks(g), blocks(beta)
    gc = jnp.cumsum(gb, axis=-1)
    incl = jnp.tril(jnp.ones((c, c), dtype=bool))
    strict = jnp.tril(jnp.ones((c, c), dtype=bool), k=-1)
    decay = jnp.exp(jnp.where(incl, gc[..., :, None] - gc[..., None, :], -jnp.inf))
    k_beta = kb * bb[..., None]
    lmat = jnp.where(strict, jnp.einsum('nbhid,nbhjd->nbhij', k_beta, kb) * decay, 0.0)
    rhs = jnp.concatenate([vb * bb[..., None], k_beta * jnp.exp(gc)[..., None]], axis=-1)
    sol = lax.linalg.triangular_solve(lmat + jnp.eye(c, dtype=F32), rhs, left_side=True,
                                      lower=True, unit_diagonal=True)
    u, w = sol[..., :dv], sol[..., dv:]
    qk = jnp.where(incl, jnp.einsum('nbhid,nbhjd->nbhij', qb, kb) * decay, 0.0)

    def step(s, xs):
        q_c, k_c, u_c, w_c, g_c, qk_c = xs
        v_new = u_c - jnp.einsum('bhck,bhkv->bhcv', w_c, s)
        o_c = (jnp.einsum('bhck,bhkv->bhcv', q_c * jnp.exp(g_c)[..., None], s)
               + jnp.einsum('bhij,bhjv->bhiv', qk_c, v_new))
        g_last = g_c[..., -1]
        s = (s * jnp.exp(g_last)[..., None, None]
             + jnp.einsum('bhck,bhcv->bhkv', k_c * jnp.exp(g_last[..., None] - g_c)[..., None], v_new))
        return s, o_c

    s_final, o = lax.scan(step, s0.astype(F32), (qb, kb, u, w, gc, qk))
    o = jnp.moveaxis(o, (0, 2), (1, 3)).reshape(bsz, n * c, nh, dv)[:, :t]
    return o.astype(q.dtype), s_final.astype(s0.dtype)


def gated_deltanet(h, conv_buf, s0, w_qkv, w_conv, w_ab, dt_bias, a_log, w_g, o_norm_g, w_out):
    bsz, t, _ = h.shape
    qkv, new_buf = causal_dwconv(h @ w_qkv, conv_buf, w_conv)
    qkv = jax.nn.silu(qkv)
    hk = GDN_HEADS * GDN_DK
    q = l2norm(qkv[..., :hk].reshape(bsz, t, GDN_HEADS, GDN_DK)) * (GDN_DK ** -0.5)
    k = l2norm(qkv[..., hk:2 * hk].reshape(bsz, t, GDN_HEADS, GDN_DK))
    v = qkv[..., 2 * hk:].reshape(bsz, t, GDN_HEADS, GDN_DV)
    ab = (h @ w_ab).astype(F32)
    beta = jax.nn.sigmoid(ab[..., :GDN_HEADS])
    g = -jnp.exp(a_log.astype(F32)) * jax.nn.softplus(ab[..., GDN_HEADS:] + dt_bias.astype(F32))
    o, s_new = chunk_gated_delta(q, k, v, g, beta, s0)
    gate = jax.nn.silu(h @ w_g).reshape(bsz, t, GDN_HEADS, GDN_DV)
    o = rmsnorm(o, o_norm_g) * gate
    return o.reshape(bsz, t, GDN_HEADS * GDN_DV) @ w_out, new_buf, s_new


def conv_ffn(h, buf, w_up, w_dw, b_dw, w_down):
    a, b = jnp.split(h @ w_up, 2, axis=-1)
    a, new_buf = causal_dwconv(a, buf, w_dw)
    return (jax.nn.silu(a + b_dw) * b) @ w_down, new_buf


def trunk(x, start_pos, pool_buf, conv_buf, gdnc_buf, gdns, ffn_buf, p):
    chunk_v, pool_new, conv_new, gdnc_new, gdns_new, ffn_new = [], [], [], [], [], []
    for i in range(DEPTH):
        j = i // N_MIXERS
        kind = i % N_MIXERS
        h = rmsnorm(x, p['norm_mix_g'][i])
        if kind == 0:
            y, vr = chunk_mlp(h, start_pos, p['a_w_in'][j], p['a_b_in'][j], p['a_ln_g'][j],
                              p['a_ln_b'][j], p['a_w_s'][j], p['a_b_s'][j], p['a_w_out'][j])
            chunk_v.append(vr)
        elif kind == 1:
            y, nb = multiscale_pool(h, pool_buf[j], start_pos, p['b_w_grp'][j], p['b_scale'][j])
            pool_new.append(nb)
        elif kind == 2:
            y, nb = conformer_conv(h, conv_buf[j], p['c_w_pw1'][j], p['c_b_pw1'][j], p['c_w_dw'][j],
                                   p['c_b_dw'][j], p['c_ln_g'][j], p['c_ln_b'][j],
                                   p['c_w_pw2'][j], p['c_b_pw2'][j])
            conv_new.append(nb)
        else:
            y, nb, ns = gated_deltanet(h, gdnc_buf[j], gdns[j], p['d_w_qkv'][j], p['d_w_conv'][j],
                                       p['d_w_ab'][j], p['d_dt_bias'][j], p['d_a_log'][j],
                                       p['d_w_g'][j], p['d_o_norm_g'][j], p['d_w_out'][j])
            gdnc_new.append(nb)
            gdns_new.append(ns)
        x = x + y
        h = rmsnorm(x, p['norm_ffn_g'][i])
        y, fb = conv_ffn(h, ffn_buf[i], p['f_w_up'][i], p['f_w_dw'][i], p['f_b_dw'][i], p['f_w_down'][i])
        ffn_new.append(fb)
        x = x + y
    x = rmsnorm(x, p['norm_final_g'])
    return (x, jnp.stack(chunk_v), jnp.stack(pool_new), jnp.stack(conv_new), jnp.stack(gdnc_new),
            jnp.stack(gdns_new), jnp.stack(ffn_new))


def setup_inputs(seed: int = 0) -> dict:
    key = jax.random.key(seed)
    ks = list(jax.random.split(key, 48))

    def nrm(shape, scale):
        return scale * jax.random.normal(ks.pop(), shape, F32)

    def gain(shape):
        return 1.0 + nrm(shape, 0.05)

    D = D_MODEL
    H = GDN_HEADS
    dt = jnp.exp(jax.random.uniform(ks.pop(), (N_D, H), F32, np.log(1e-3), np.log(1e-1)))
    a_init = jax.random.uniform(ks.pop(), (N_D, H), F32, 1.0, 16.0)
    return {
        'x_prompt': nrm((BATCH, SEQ, D), 1.0),
        'x_sample': nrm((DEC_BATCH, DEC_SEQ, D), 1.0),
        'state_pool': nrm((N_B, DEC_BATCH, POOL_BUF, D), 1.0),
        'state_conv': nrm((N_C, DEC_BATCH, CONV_BUF, D_C), 0.5),
        'state_gdn_conv': nrm((N_D, DEC_BATCH, GDN_BUF, GDN_QKV), 1.0),
        'state_gdn_S': nrm((N_D, DEC_BATCH, H, GDN_DK, GDN_DV), 0.3),
        'state_ffn_conv': nrm((DEPTH, DEC_BATCH, FFN_BUF, D_FF), 1.0),
        'norm_mix_g': gain((DEPTH, D)),
        'norm_ffn_g': gain((DEPTH, D)),
        'norm_final_g': gain((D,)),
        'a_w_in': nrm((N_A, D, 2 * D_A), D ** -0.5),
        'a_b_in': nrm((N_A, 2 * D_A), 0.02),
        'a_ln_g': gain((N_A, D_A)),
        'a_ln_b': nrm((N_A, D_A), 0.02),
        'a_w_s': nrm((N_A, A_GROUPS, CHUNK, CHUNK), CHUNK ** -0.5),
        'a_b_s': gain((N_A, A_GROUPS, CHUNK)),
        'a_w_out': nrm((N_A, D_A, D), 0.5 * D_A ** -0.5),
        'b_w_grp': nrm((N_B, POOL_GROUPS, POOL_GDIM, POOL_GDIM), POOL_GDIM ** -0.5),
        'b_scale': 0.3 + nrm((N_B, D), 0.05),
        'c_w_pw1': nrm((N_C, D, 2 * D_C), D ** -0.5),
        'c_b_pw1': nrm((N_C, 2 * D_C), 0.02),
        'c_w_dw': nrm((N_C, CONV_WIDTH, D_C), CONV_WIDTH ** -0.5),
        'c_b_dw': nrm((N_C, D_C), 0.02),
        'c_ln_g': gain((N_C, D_C)),
        'c_ln_b': nrm((N_C, D_C), 0.02),
        'c_w_pw2': nrm((N_C, D_C, D), 0.5 * D_C ** -0.5),
        'c_b_pw2': nrm((N_C, D), 0.02),
        'd_w_qkv': nrm((N_D, D, GDN_QKV), D ** -0.5),
        'd_w_conv': nrm((N_D, GDN_CONV, GDN_QKV), 0.5),
        'd_w_ab': nrm((N_D, D, 2 * H), 0.5 * D ** -0.5),
        'd_dt_bias': dt + jnp.log(-jnp.expm1(-dt)),
        'd_a_log': jnp.log(a_init),
        'd_w_g': nrm((N_D, D, H * GDN_DV), D ** -0.5),
        'd_o_norm_g': gain((N_D, GDN_DV)),
        'd_w_out': nrm((N_D, H * GDN_DV, D), 0.5 * (H * GDN_DV) ** -0.5),
        'f_w_up': nrm((DEPTH, D, 2 * D_FF), D ** -0.5),
        'f_w_dw': nrm((DEPTH, FFN_CONV, D_FF), FFN_CONV ** -0.5),
        'f_b_dw': nrm((DEPTH, D_FF), 0.02),
        'f_w_down': nrm((DEPTH, D_FF, D), 0.5 * D_FF ** -0.5),
    }


def reference(x_prompt, x_sample, state_pool, state_conv, state_gdn_conv, state_gdn_S, state_ffn_conv,
              norm_mix_g, norm_ffn_g, norm_final_g,
              a_w_in, a_b_in, a_ln_g, a_ln_b, a_w_s, a_b_s, a_w_out,
              b_w_grp, b_scale,
              c_w_pw1, c_b_pw1, c_w_dw, c_b_dw, c_ln_g, c_ln_b, c_w_pw2, c_b_pw2,
              d_w_qkv, d_w_conv, d_w_ab, d_dt_bias, d_a_log, d_w_g, d_o_norm_g, d_w_out,
              f_w_up, f_w_dw, f_b_dw, f_w_down):
    p = dict(norm_mix_g=norm_mix_g, norm_ffn_g=norm_ffn_g, norm_final_g=norm_final_g,
             a_w_in=a_w_in, a_b_in=a_b_in, a_ln_g=a_ln_g, a_ln_b=a_ln_b, a_w_s=a_w_s, a_b_s=a_b_s,
             a_w_out=a_w_out, b_w_grp=b_w_grp, b_scale=b_scale,
             c_w_pw1=c_w_pw1, c_b_pw1=c_b_pw1, c_w_dw=c_w_dw, c_b_dw=c_b_dw, c_ln_g=c_ln_g,
             c_ln_b=c_ln_b, c_w_pw2=c_w_pw2, c_b_pw2=c_b_pw2,
             d_w_qkv=d_w_qkv, d_w_conv=d_w_conv, d_w_ab=d_w_ab, d_dt_bias=d_dt_bias, d_a_log=d_a_log,
             d_w_g=d_w_g, d_o_norm_g=d_o_norm_g, d_w_out=d_w_out,
             f_w_up=f_w_up, f_w_dw=f_w_dw, f_b_dw=f_b_dw, f_w_down=f_w_down)
    bp = x_prompt.shape[0]
    dt = x_prompt.dtype
    (y_prompt, p_chunk_v, p_pool, p_conv, p_gdn_conv, p_gdn_S, p_ffn) = trunk(
        x_prompt, 0,
        jnp.zeros((N_B, bp, POOL_BUF, D_MODEL), dt),
        jnp.zeros((N_C, bp, CONV_BUF, D_C), dt),
        jnp.zeros((N_D, bp, GDN_BUF, GDN_QKV), dt),
        jnp.zeros((N_D, bp, GDN_HEADS, GDN_DK, GDN_DV), dt),
        jnp.zeros((DEPTH, bp, FFN_BUF, D_FF), dt), p)
    (y_sample, s_chunk_v, s_pool, s_conv, s_gdn_conv, s_gdn_S, s_ffn) = trunk(
        x_sample, PAST_LEN, state_pool, state_conv, state_gdn_conv, state_gdn_S, state_ffn_conv, p)
    return (y_prompt, y_sample, p_chunk_v, s_chunk_v, p_pool, s_pool, p_conv, s_conv,
            p_gdn_conv, s_gdn_conv, p_gdn_S, s_gdn_S, p_ffn, s_ffn)
```

```python
import functools

import numpy as np
import jax
import jax.numpy as jnp
from jax import lax
from jax.experimental import pallas as pl
from jax.experimental.pallas import tpu as pltpu

F32 = jnp.float32
BF16 = jnp.bfloat16

D_MODEL = 1024
BATCH = 8
SEQ = 2048
DEPTH = 4
DEC_BATCH = 128
DEC_SEQ = 8
PAST_LEN = 16384
EPS = 1e-6
CHUNK = 128
A_GROUPS = 8
A_GDIM = D_MODEL // A_GROUPS
POOL_WINDOWS = (2, 4, 8, 16)
POOL_GDIM = D_MODEL // len(POOL_WINDOWS)
POOL_BUF = max(POOL_WINDOWS) - 1
CONV_WIDTH = 31
CONV_BUF = CONV_WIDTH - 1
GDN_HEADS = 8
GDN_DK = 128
GDN_DV = 128
GDN_CONV = 4
GDN_BUF = GDN_CONV - 1
GDN_QKV = GDN_HEADS * (2 * GDN_DK + GDN_DV)
D_FF = 2816
FFN_CONV = 3
FFN_BUF = FFN_CONV - 1

SUBLANES = 8
LANES = 128
VMEM_LIMIT_BYTES = 60 * 1024 * 1024

FFN_FC = 256
FFN_NFC = D_FF // FFN_FC
SAMPLE_BB = 32
SAMPLE_M = DEC_SEQ * SAMPLE_BB
SAMPLE_NB = DEC_BATCH // SAMPLE_BB
GDN_R = 128
GDN_SB = GDN_R // DEC_SEQ
AB_PAD = LANES

assert PAST_LEN % CHUNK == 0 and DEC_SEQ <= CHUNK and SEQ % CHUNK == 0 and SEQ % GDN_R == 0


def _round_up(n, m):
    return (n + m - 1) // m * m


def _rms(x, g):
    return x * lax.rsqrt(jnp.mean(x * x, axis=-1, keepdims=True) + EPS) * g


def _ln(x, g, b):
    mu = jnp.mean(x, axis=-1, keepdims=True)
    xc = x - mu
    var = jnp.mean(xc * xc, axis=-1, keepdims=True)
    return xc * lax.rsqrt(var + EPS) * g + b


def _sigmoid(x):
    return 1.0 / (1.0 + jnp.exp(-x))


def _silu(x):
    return x * _sigmoid(x)


def _gelu_tanh(x):
    c = float(np.sqrt(2.0 / np.pi))
    return x * (0.5 * (1.0 + jnp.tanh(c * (x + 0.044715 * (x * x * x)))))


def _softplus(x):
    return jnp.maximum(x, 0.0) + jnp.log1p(jnp.exp(-jnp.abs(x)))


def _mm(a, b):
    return jnp.dot(a, b, preferred_element_type=F32)


def _dot1(a, b):
    return _mm(a.astype(BF16), b.astype(BF16))


def _dot_nt(a, b):
    return lax.dot_general(a.astype(BF16), b.astype(BF16), (((1,), (1,)), ((), ())),
                           preferred_element_type=F32)


def _dot_tn(a, b):
    return lax.dot_general(a.astype(BF16), b.astype(BF16), (((0,), (0,)), ((), ())),
                           preferred_element_type=F32)


def _split2(a):
    hi = a.astype(BF16)
    lo = (a - hi.astype(F32)).astype(BF16)
    return hi, lo


def _dot3(a, b):
    ah, al = _split2(a)
    bh, bl = _split2(b)
    return _mm(ah, bh) + (_mm(al, bh) + _mm(ah, bl))


def _cumsum_rows(tri_bf16, g):
    hi = g.astype(BF16)
    r = g - hi.astype(F32)
    mid = r.astype(BF16)
    lo = (r - mid.astype(F32)).astype(BF16)
    return _mm(tri_bf16, hi) + (_mm(tri_bf16, mid) + _mm(tri_bf16, lo))


def _resident(shape):
    nd = len(shape)
    return pl.BlockSpec(shape, lambda *_: (0,) * nd, pipeline_mode=pl.Buffered(1))


def _params(n_axes):
    return pltpu.CompilerParams(dimension_semantics=("arbitrary",) * n_axes,
                                vmem_limit_bytes=VMEM_LIMIT_BYTES)


def _prompt_tile_spec(m, c):
    return pl.BlockSpec((None, m, c), lambda b, t: (b, t, 0))


def _prompt_seq_spec(r, c):
    return pl.BlockSpec((None, r, c), lambda b, t: (b, 0, 0))


def _sample_spec(r, c):
    return pl.BlockSpec((None, r, c), lambda j: (j, 0, 0))


def _ffn_kernel(*refs, sample, last, m, st, hp):
    it = iter(refs)
    x_ref = next(it)
    st_ref = next(it) if sample else None
    g_ref, wup_ref, wdw_ref, bdw_ref, wdown_ref = (next(it) for _ in range(5))
    fg_ref = next(it) if last else None
    xo_ref = next(it)
    sto_ref = next(it)
    yf_ref = next(it) if last else None
    h_ref, acc_ref, abuf = next(it), next(it), next(it)
    halo = None if sample else next(it)

    x = x_ref[...]
    h_ref[...] = _rms(x, g_ref[...]).astype(BF16)
    acc_ref[...] = x
    if not sample:
        @pl.when(pl.program_id(1) == 0)
        def _():
            halo[...] = jnp.zeros_like(halo)

    def body(f, carry):
        h = h_ref[...]
        a = _mm(h, wup_ref[f])
        b = _mm(h, wup_ref[f + FFN_NFC])
        abuf[0:hp, :] = st_ref[f] if sample else halo[f]
        abuf[hp:hp + m, :] = a
        w = wdw_ref[f]
        conv = (w[0:1] * abuf[hp - 2 * st:hp - 2 * st + m, :]
                + w[1:2] * abuf[hp - st:hp - st + m, :] + w[2:3] * a)
        new_halo = abuf[m:m + hp, :]
        sto_ref[f] = new_halo
        if not sample:
            halo[f] = new_halo
        y = _silu(conv + bdw_ref[f]) * b
        acc_ref[...] += _mm(y.astype(BF16), wdown_ref[f])
        return carry

    lax.fori_loop(0, FFN_NFC, body, 0)
    out = acc_ref[...]
    xo_ref[...] = out
    if last:
        yf_ref[...] = _rms(out, fg_ref[...])


def _ffn_call(x, state, g, wup, wdw, bdw, wdown, final_g, *, sample):
    last = final_g is not None
    d = D_MODEL
    if sample:
        nb, m, _ = x.shape
        st, hp = SAMPLE_BB, FFN_BUF * SAMPLE_BB
        grid = (nb,)
        x_spec = _sample_spec(m, d)
        st_spec = pl.BlockSpec((None, FFN_NFC, hp, FFN_FC), lambda j: (j, 0, 0, 0))
        st_shape = (nb, FFN_NFC, hp, FFN_FC)
    else:
        bsz, t, _ = x.shape
        m, st, hp = 512, 1, SUBLANES
        grid = (bsz, t // m)
        x_spec = _prompt_tile_spec(m, d)
        st_spec = pl.BlockSpec((None, FFN_NFC, hp, FFN_FC), lambda b, t: (b, 0, 0, 0))
        st_shape = (bsz, FFN_NFC, hp, FFN_FC)
    in_specs = [x_spec] + ([st_spec] if sample else []) + [
        _resident((1, d)), _resident(wup.shape), _resident(wdw.shape), _resident(bdw.shape),
        _resident(wdown.shape)] + ([_resident((1, d))] if last else [])
    args = [x] + ([state] if sample else []) + [g, wup, wdw, bdw, wdown] + ([final_g] if last else [])
    out_specs = [x_spec, st_spec] + ([x_spec] if last else [])
    out_shape = [jax.ShapeDtypeStruct(x.shape, F32), jax.ShapeDtypeStruct(st_shape, F32)] + (
        [jax.ShapeDtypeStruct(x.shape, F32)] if last else [])
    scratch = [pltpu.VMEM((m, d), BF16), pltpu.VMEM((m, d), F32), pltpu.VMEM((hp + m, FFN_FC), F32)]
    if not sample:
        scratch.append(pltpu.VMEM((FFN_NFC, hp, FFN_FC), F32))
    return pl.pallas_call(
        functools.partial(_ffn_kernel, sample=sample, last=last, m=m, st=st, hp=hp),
        grid=grid, in_specs=in_specs, out_specs=out_specs, out_shape=out_shape,
        scratch_shapes=scratch, compiler_params=_params(len(grid)),
        name="ffn_sample" if sample else "ffn_prompt")(*args)


def _mixa_prompt_kernel(x_ref, g_ref, win_ref, bin_ref, lng_ref, lnb_ref, ws_ref, bsx_ref, wout_ref,
                        xo_ref, vo_ref, h_ref, vb_ref, z_ref, *, m):
    d = D_MODEL
    x = x_ref[...]
    h_ref[...] = _rms(x, g_ref[...]).astype(BF16)
    vv = _gelu_tanh(_mm(h_ref[...], win_ref[:, d:2 * d]) + bin_ref[:, d:2 * d])
    v = _ln(vv, lng_ref[...], lnb_ref[...])

    @pl.when(pl.program_id(1) == pl.num_programs(1) - 1)
    def _():
        vo_ref[...] = v[m - CHUNK:m, :]

    vb_ref[...] = v.astype(BF16)
    row = lax.broadcasted_iota(jnp.int32, (CHUNK, CHUNK), 0)
    col = lax.broadcasted_iota(jnp.int32, (CHUNK, CHUNK), 1)
    causal = col <= row
    for grp in range(A_GROUPS):
        cs = slice(grp * A_GDIM, (grp + 1) * A_GDIM)
        wsg = jnp.where(causal, ws_ref[grp], 0.0).astype(BF16)
        u = _gelu_tanh(_mm(h_ref[...], win_ref[:, cs]) + bin_ref[:, cs])
        for c in range(m // CHUNK):
            rs = slice(c * CHUNK, (c + 1) * CHUNK)
            mixed = _mm(wsg, vb_ref[rs, cs]) + bsx_ref[:, cs]
            z_ref[rs, cs] = (u[rs, :] * mixed).astype(BF16)
    xo_ref[...] = x + _mm(z_ref[...], wout_ref[...])


def _mixa_sample_kernel(x_ref, g_ref, win_ref, bin_ref, lng_ref, lnb_ref, wsx_ref, bsx_ref, wout_ref,
                        xo_ref, vo_ref, z_ref, *, bb):
    d = D_MODEL
    x = x_ref[...]
    h = _rms(x, g_ref[...]).astype(BF16)
    u = _gelu_tanh(_mm(h, win_ref[:, 0:d]) + bin_ref[:, 0:d])
    vv = _gelu_tanh(_mm(h, win_ref[:, d:2 * d]) + bin_ref[:, d:2 * d])
    v = _ln(vv, lng_ref[...], lnb_ref[...])
    vo_ref[...] = v
    for t in range(DEC_SEQ):
        mixed = bsx_ref[t:t + 1, :]
        for j in range(t + 1):
            mixed = mixed + wsx_ref[t * DEC_SEQ + j:t * DEC_SEQ + j + 1, :] * v[j * bb:(j + 1) * bb, :]
        z_ref[t * bb:(t + 1) * bb, :] = (u[t * bb:(t + 1) * bb, :] * mixed).astype(BF16)
    xo_ref[...] = x + _mm(z_ref[...], wout_ref[...])


def _mixa_prompt_call(x, g, win, bin_, lng, lnb, ws, bsx, wout):
    bsz, t, d = x.shape
    m = 512
    grid = (bsz, t // m)
    return pl.pallas_call(
        functools.partial(_mixa_prompt_kernel, m=m), grid=grid,
        in_specs=[_prompt_tile_spec(m, d), _resident((1, d)), _resident(win.shape), _resident(bin_.shape),
                  _resident((1, d)), _resident((1, d)), _resident(ws.shape), _resident(bsx.shape),
                  _resident(wout.shape)],
        out_specs=[_prompt_tile_spec(m, d), _prompt_seq_spec(CHUNK, d)],
        out_shape=[jax.ShapeDtypeStruct(x.shape, F32), jax.ShapeDtypeStruct((bsz, CHUNK, d), F32)],
        scratch_shapes=[pltpu.VMEM((m, d), BF16), pltpu.VMEM((m, d), BF16), pltpu.VMEM((m, d), BF16)],
        compiler_params=_params(2), name="mixa_prompt")(x, g, win, bin_, lng, lnb, ws, bsx, wout)


def _mixa_sample_call(x, g, win, bin_, lng, lnb, wsx, bsx, wout):
    nb, m, d = x.shape
    return pl.pallas_call(
        functools.partial(_mixa_sample_kernel, bb=SAMPLE_BB), grid=(nb,),
        in_specs=[_sample_spec(m, d), _resident((1, d)), _resident(win.shape), _resident(bin_.shape),
                  _resident((1, d)), _resident((1, d)), _resident(wsx.shape), _resident(bsx.shape),
                  _resident(wout.shape)],
        out_specs=[_sample_spec(m, d), _sample_spec(m, d)],
        out_shape=[jax.ShapeDtypeStruct(x.shape, F32), jax.ShapeDtypeStruct(x.shape, F32)],
        scratch_shapes=[pltpu.VMEM((m, d), BF16)],
        compiler_params=_params(1), name="mixa_sample")(x, g, win, bin_, lng, lnb, wsx, bsx, wout)


def _pool_kernel(*refs, sample, m, st, hp):
    it = iter(refs)
    x_ref = next(it)
    st_ref = next(it) if sample else None
    g_ref, wg_ref, sc_ref = next(it), next(it), next(it)
    xo_ref, sto_ref = next(it), next(it)
    hbuf = next(it)
    halo = None if sample else next(it)

    x = x_ref[...]
    h = _rms(x, g_ref[...])
    if sample:
        hbuf[0:hp, :] = st_ref[...]
        pos0 = PAST_LEN
    else:
        @pl.when(pl.program_id(1) == 0)
        def _():
            halo[...] = jnp.zeros_like(halo)
        hbuf[0:hp, :] = halo[...]
        pos0 = pl.program_id(1) * m
    hbuf[hp:hp + m, :] = h
    new_halo = hbuf[m:m + hp, :]
    sto_ref[...] = new_halo
    if not sample:
        halo[...] = new_halo
    pos = pos0 + lax.broadcasted_iota(jnp.int32, (m, POOL_GDIM), 0) // st
    for gi, win in enumerate(POOL_WINDOWS):
        cs = slice(gi * POOL_GDIM, (gi + 1) * POOL_GDIM)
        s = h[:, cs]
        for j in range(1, win):
            s = s + hbuf[hp - j * st:hp - j * st + m, cs]
        cnt = jnp.minimum(pos + 1, win).astype(F32)
        pooled = s / cnt - h[:, cs]
        y = _dot1(pooled, wg_ref[gi]) * sc_ref[:, cs]
        xo_ref[:, cs] = x[:, cs] + y


def _pool_call(x, state, g, wg, sc, *, sample):
    d = D_MODEL
    if sample:
        nb, m, _ = x.shape
        st, hp = SAMPLE_BB, POOL_BUF * SAMPLE_BB
        grid = (nb,)
        x_spec, st_spec = _sample_spec(m, d), _sample_spec(hp, d)
        st_shape = (nb, hp, d)
    else:
        bsz, t, _ = x.shape
        m, st, hp = 512, 1, _round_up(POOL_BUF, SUBLANES)
        grid = (bsz, t // m)
        x_spec, st_spec = _prompt_tile_spec(m, d), _prompt_seq_spec(hp, d)
        st_shape = (bsz, hp, d)
    in_specs = [x_spec] + ([st_spec] if sample else []) + [
        _resident((1, d)), _resident(wg.shape), _resident((1, d))]
    args = [x] + ([state] if sample else []) + [g, wg, sc]
    scratch = [pltpu.VMEM((hp + m, d), F32)] + ([] if sample else [pltpu.VMEM((hp, d), F32)])
    return pl.pallas_call(
        functools.partial(_pool_kernel, sample=sample, m=m, st=st, hp=hp), grid=grid,
        in_specs=in_specs, out_specs=[x_spec, st_spec],
        out_shape=[jax.ShapeDtypeStruct(x.shape, F32), jax.ShapeDtypeStruct(st_shape, F32)],
        scratch_shapes=scratch, compiler_params=_params(len(grid)),
        name="pool_sample" if sample else "pool_prompt")(*args)


CONV_RB = 64


def _conf_kernel(*refs, sample, m, st, hp):
    it = iter(refs)
    x_ref = next(it)
    st_ref = next(it) if sample else None
    (g_ref, w1_ref, b1_ref, wdw_ref, bdw_ref, lng_ref, lnb_ref, w2_ref, b2_ref) = (next(it) for _ in range(9))
    xo_ref, sto_ref = next(it), next(it)
    zbuf, cbuf = next(it), next(it)
    halo = None if sample else next(it)
    d = D_MODEL

    x = x_ref[...]
    h = _rms(x, g_ref[...]).astype(BF16)
    a = _mm(h, w1_ref[:, 0:d]) + b1_ref[:, 0:d]
    gt = _mm(h, w1_ref[:, d:2 * d]) + b1_ref[:, d:2 * d]
    z = a * _sigmoid(gt)
    if sample:
        zbuf[0:hp, :] = st_ref[...]
    else:
        @pl.when(pl.program_id(1) == 0)
        def _():
            halo[...] = jnp.zeros_like(halo)
        zbuf[0:hp, :] = halo[...]
    zbuf[hp:hp + m, :] = z
    new_halo = zbuf[m:m + hp, :]
    sto_ref[...] = new_halo
    if not sample:
        halo[...] = new_halo
    for c0 in range(0, d, LANES):
        cs = slice(c0, c0 + LANES)
        for r0 in range(0, m, CONV_RB):
            acc = wdw_ref[CONV_WIDTH - 1:CONV_WIDTH, cs] * zbuf[hp + r0:hp + r0 + CONV_RB, cs]
            for k in range(CONV_WIDTH - 1):
                off = hp - (CONV_WIDTH - 1 - k) * st + r0
                acc = acc + wdw_ref[k:k + 1, cs] * zbuf[off:off + CONV_RB, cs]
            cbuf[r0:r0 + CONV_RB, cs] = acc
    c = _silu(_ln(cbuf[...] + bdw_ref[...], lng_ref[...], lnb_ref[...]))
    xo_ref[...] = x + _mm(c.astype(BF16), w2_ref[...]) + b2_ref[...]


def _conf_call(x, state, g, w1, b1, wdw, bdw, lng, lnb, w2, b2, *, sample):
    d = D_MODEL
    if sample:
        nb, m, _ = x.shape
        st, hp = SAMPLE_BB, CONV_BUF * SAMPLE_BB
        grid = (nb,)
        x_spec, st_spec = _sample_spec(m, d), _sample_spec(hp, d)
        st_shape = (nb, hp, d)
    else:
        bsz, t, _ = x.shape
        m, st, hp = 256, 1, _round_up(CONV_BUF, SUBLANES)
        grid = (bsz, t // m)
        x_spec, st_spec = _prompt_tile_spec(m, d), _prompt_seq_spec(hp, d)
        st_shape = (bsz, hp, d)
    in_specs = [x_spec] + ([st_spec] if sample else []) + [
        _resident((1, d)), _resident(w1.shape), _resident(b1.shape), _resident(wdw.shape),
        _resident((1, d)), _resident((1, d)), _resident((1, d)), _resident(w2.shape), _resident((1, d))]
    args = [x] + ([state] if sample else []) + [g, w1, b1, wdw, bdw, lng, lnb, w2, b2]
    scratch = [pltpu.VMEM((hp + m, d), F32), pltpu.VMEM((m, d), F32)] + (
        [] if sample else [pltpu.VMEM((hp, d), F32)])
    return pl.pallas_call(
        functools.partial(_conf_kernel, sample=sample, m=m, st=st, hp=hp), grid=grid,
        in_specs=in_specs, out_specs=[x_spec, st_spec],
        out_shape=[jax.ShapeDtypeStruct(x.shape, F32), jax.ShapeDtypeStruct(st_shape, F32)],
        scratch_shapes=scratch, compiler_params=_params(len(grid)),
        name="conf_sample" if sample else "conf_prompt")(*args)


def _gdn_project(h, conv, wab_ref, alog_ref, dt_ref, wg_ref, q_scr, k_scr, v_scr, bg_scr, gate_scr):
    hk = GDN_HEADS * GDN_DK
    qkv = _silu(conv)
    for hd in range(GDN_HEADS):
        cs = slice(hd * GDN_DK, (hd + 1) * GDN_DK)
        qh = qkv[:, hd * GDN_DK:(hd + 1) * GDN_DK]
        kh = qkv[:, hk + hd * GDN_DK:hk + (hd + 1) * GDN_DK]
        q_scr[:, cs] = qh * lax.rsqrt(jnp.sum(qh * qh, axis=-1, keepdims=True) + EPS) * (GDN_DK ** -0.5)
        k_scr[:, cs] = kh * lax.rsqrt(jnp.sum(kh * kh, axis=-1, keepdims=True) + EPS)
    v_scr[...] = qkv[:, 2 * hk:]
    ab = _mm(h, wab_ref[...])
    lane = lax.broadcasted_iota(jnp.int32, ab.shape, 1)
    beta = _sigmoid(ab)
    gl = -jnp.exp(alog_ref[...]) * _softplus(ab + dt_ref[...])
    bg_scr[...] = jnp.where(lane < GDN_HEADS, beta, gl)
    gate_scr[...] = _silu(_mm(h, wg_ref[...]))


def _wy_head(qh, kh, vh, beta_c, gc_c, gc_r, incl, strict, eye, levels):
    dec = jnp.where(incl, jnp.exp(jnp.where(incl, gc_c - gc_r, 0.0)), 0.0)
    kb = kh * beta_c
    a = jnp.where(strict, _dot_nt(kb, kh) * dec, 0.0)
    qk = _dot_nt(qh, kh) * dec
    p = eye - a
    apow = a
    for _ in range(levels):
        apow = _dot3(apow, apow)
        p = p + _dot3(p, apow)
    rhs = jnp.concatenate([vh * beta_c, kb * jnp.exp(gc_c)], axis=1)
    sol = _dot3(p, rhs)
    return sol[:, :GDN_DV], sol[:, GDN_DV:], qk


def _gdn_finish(x, o_scr, gate_scr, og_ref, wout_ref):
    for hd in range(GDN_HEADS):
        cs = slice(hd * GDN_DV, (hd + 1) * GDN_DV)
        oh = o_scr[:, cs]
        on = oh * lax.rsqrt(jnp.mean(oh * oh, axis=-1, keepdims=True) + EPS) * og_ref[...]
        o_scr[:, cs] = on * gate_scr[:, cs]
    return x + _mm(o_scr[...].astype(BF16), wout_ref[...])


def _gdn_prompt_kernel(x_ref, g_ref, wqkv_ref, wc_ref, wab_ref, alog_ref, dt_ref, wg_ref, og_ref, wout_ref,
                       xo_ref, cst_ref, so_ref,
                       cbuf, halo, s_scr, q_scr, k_scr, v_scr, bg_scr, gate_scr, o_scr, *, m):
    hp = SUBLANES
    r = GDN_R

    @pl.when(pl.program_id(1) == 0)
    def _():
        halo[...] = jnp.zeros_like(halo)
        s_scr[...] = jnp.zeros_like(s_scr)

    x = x_ref[...]
    h = _rms(x, g_ref[...]).astype(BF16)
    pre = _mm(h, wqkv_ref[...])
    cbuf[0:hp, :] = halo[...]
    cbuf[hp:hp + m, :] = pre
    conv = wc_ref[GDN_CONV - 1:GDN_CONV, :] * pre
    for k in range(GDN_CONV - 1):
        off = hp - (GDN_CONV - 1 - k)
        conv = conv + wc_ref[k:k + 1, :] * cbuf[off:off + m, :]
    new_halo = cbuf[m:m + hp, :]
    halo[...] = new_halo
    cst_ref[...] = new_halo
    _gdn_project(h, conv, wab_ref, alog_ref, dt_ref, wg_ref, q_scr, k_scr, v_scr, bg_scr, gate_scr)

    row = lax.broadcasted_iota(jnp.int32, (r, r), 0)
    col = lax.broadcasted_iota(jnp.int32, (r, r), 1)
    incl = col <= row
    strict = col < row
    eye = jnp.where(col == row, 1.0, 0.0).astype(F32)
    tri = jnp.where(incl, 1.0, 0.0).astype(BF16)
    for c in range(m // r):
        rs = slice(c * r, (c + 1) * r)
        bg = bg_scr[rs, :]
        gc = _cumsum_rows(tri, bg)
        gct = gc.T
        for hd in range(GDN_HEADS):
            cs = slice(hd * GDN_DK, (hd + 1) * GDN_DK)
            gcol = GDN_HEADS + hd
            gc_c = gc[:, gcol:gcol + 1]
            gc_r = gct[gcol:gcol + 1, :]
            qh, kh = q_scr[rs, cs], k_scr[rs, cs]
            u, w, qk = _wy_head(qh, kh, v_scr[rs, cs], bg[:, hd:hd + 1], gc_c, gc_r,
                                incl, strict, eye, 6)
            s = s_scr[hd]
            wq = jnp.concatenate([w, qh * jnp.exp(gc_c)], axis=0)
            ws = _dot1(wq, s)
            v_new = u - ws[0:r, :]
            o_scr[rs, cs] = ws[r:2 * r, :] + _dot1(qk, v_new)
            g_last = gc_c[r - 1:r, :]
            s_scr[hd] = s * jnp.exp(g_last) + _dot_tn(kh * jnp.exp(g_last - gc_c), v_new)
    xo_ref[...] = _gdn_finish(x, o_scr, gate_scr, og_ref, wout_ref)
    so_ref[...] = s_scr[...]


def _gdn_sample_kernel(x_ref, p_ref, s_ref, g_ref, wqkv_ref, wc_ref, wab_ref, alog_ref, dt_ref, wg_ref,
                       og_ref, wout_ref,
                       xo_ref, pre_ref, so_ref,
                       cbuf, q_scr, k_scr, v_scr, bg_scr, gate_scr, o_scr,
                       gc_scr, u_scr, w_scr, qe_scr, vn_scr, oi_scr):
    r = GDN_R
    t_len = DEC_SEQ
    x = x_ref[...]
    h = _rms(x, g_ref[...]).astype(BF16)
    pre = _mm(h, wqkv_ref[...])
    pre_ref[...] = pre
    cbuf[:, 0:t_len, :] = p_ref[...].reshape(GDN_SB, t_len, GDN_QKV)
    cbuf[:, t_len:2 * t_len, :] = pre.reshape(GDN_SB, t_len, GDN_QKV)
    conv = wc_ref[GDN_CONV - 1:GDN_CONV, :] * pre
    for k in range(GDN_CONV - 1):
        off = t_len - (GDN_CONV - 1 - k)
        conv = conv + wc_ref[k:k + 1, :] * cbuf[:, off:off + t_len, :].reshape(r, GDN_QKV)
    _gdn_project(h, conv, wab_ref, alog_ref, dt_ref, wg_ref, q_scr, k_scr, v_scr, bg_scr, gate_scr)

    row = lax.broadcasted_iota(jnp.int32, (r, r), 0)
    col = lax.broadcasted_iota(jnp.int32, (r, r), 1)
    same = (row // t_len) == (col // t_len)
    incl = same & (col <= row)
    strict = same & (col < row)
    eye = jnp.where(col == row, 1.0, 0.0).astype(F32)
    tri = jnp.where(incl, 1.0, 0.0).astype(BF16)
    bg = bg_scr[...]
    gc = _cumsum_rows(tri, bg)
    gc_scr[...] = gc
    gct = gc.T
    zeros8 = jnp.zeros((t_len, GDN_DK), F32)
    for hd in range(GDN_HEADS):
        cs = slice(hd * GDN_DK, (hd + 1) * GDN_DK)
        gcol = GDN_HEADS + hd
        gc_c = gc[:, gcol:gcol + 1]
        gc_r = gct[gcol:gcol + 1, :]
        qh, kh = q_scr[:, cs], k_scr[:, cs]
        u, w, qk = _wy_head(qh, kh, v_scr[:, cs], bg[:, hd:hd + 1], gc_c, gc_r, incl, strict, eye, 2)
        u_scr[...] = u
        w_scr[...] = w
        qe_scr[...] = qh * jnp.exp(gc_c)

        def seq_body(b, carry, hd=hd, cs=cs, gcol=gcol):
            r0 = pl.multiple_of(b * t_len, t_len)
            rows = pl.ds(r0, t_len)
            s = s_ref[b, hd]
            wq = jnp.concatenate([w_scr[rows, :], qe_scr[rows, :]], axis=0)
            ws = _dot1(wq, s)
            v_new = u_scr[rows, :] - ws[0:t_len, :]
            vn_scr[rows, :] = v_new
            oi_scr[rows, :] = ws[t_len:2 * t_len, :]
            gcb = gc_scr[rows, :][:, gcol:gcol + 1]
            g_last = gcb[t_len - 1:t_len, :]
            kd = k_scr[rows, cs] * jnp.exp(g_last - gcb)
            kd16 = jnp.concatenate([kd, zeros8], axis=0)
            vn16 = jnp.concatenate([v_new, zeros8], axis=0)
            so_ref[b, hd] = s * jnp.exp(g_last) + _dot_tn(kd16, vn16)
            return carry

        lax.fori_loop(0, GDN_SB, seq_body, 0)
        o_scr[:, cs] = oi_scr[...] + _dot1(qk, vn_scr[...])
    xo_ref[...] = _gdn_finish(x, o_scr, gate_scr, og_ref, wout_ref)


def _gdn_weight_specs(wqkv, wc, wab, wg, wout):
    d = D_MODEL
    return [_resident((1, d)), _resident(wqkv.shape), _resident(wc.shape), _resident(wab.shape),
            _resident((1, AB_PAD)), _resident((1, AB_PAD)), _resident(wg.shape),
            _resident((1, GDN_DV)), _resident(wout.shape)]


def _gdn_prompt_call(x, g, wqkv, wc, wab, alog, dt, wg, og, wout):
    bsz, t, d = x.shape
    m = 256
    hv = GDN_HEADS * GDN_DV
    grid = (bsz, t // m)
    s_spec = pl.BlockSpec((None, GDN_HEADS, GDN_DK, GDN_DV), lambda b, t: (b, 0, 0, 0))
    return pl.pallas_call(
        functools.partial(_gdn_prompt_kernel, m=m), grid=grid,
        in_specs=[_prompt_tile_spec(m, d)] + _gdn_weight_specs(wqkv, wc, wab, wg, wout),
        out_specs=[_prompt_tile_spec(m, d), _prompt_seq_spec(SUBLANES, GDN_QKV), s_spec],
        out_shape=[jax.ShapeDtypeStruct(x.shape, F32),
                   jax.ShapeDtypeStruct((bsz, SUBLANES, GDN_QKV), F32),
                   jax.ShapeDtypeStruct((bsz, GDN_HEADS, GDN_DK, GDN_DV), F32)],
        scratch_shapes=[pltpu.VMEM((SUBLANES + m, GDN_QKV), F32), pltpu.VMEM((SUBLANES, GDN_QKV), F32),
                        pltpu.VMEM((GDN_HEADS, GDN_DK, GDN_DV), F32),
                        pltpu.VMEM((m, hv), F32), pltpu.VMEM((m, hv), F32), pltpu.VMEM((m, hv), F32),
                        pltpu.VMEM((m, AB_PAD), F32), pltpu.VMEM((m, hv), F32), pltpu.VMEM((m, hv), F32)],
        compiler_params=_params(2), name="gdn_prompt")(x, g, wqkv, wc, wab, alog, dt, wg, og, wout)


def _gdn_sample_call(x, p, s0, g, wqkv, wc, wab, alog, dt, wg, og, wout):
    rows, d = x.shape
    r = GDN_R
    hv = GDN_HEADS * GDN_DV
    grid = (rows // r,)
    row_spec = lambda c: pl.BlockSpec((r, c), lambda j: (j, 0))
    s_spec = pl.BlockSpec((GDN_SB, GDN_HEADS, GDN_DK, GDN_DV), lambda j: (j, 0, 0, 0))
    tile = lambda c: pltpu.VMEM((r, c), F32)
    return pl.pallas_call(
        _gdn_sample_kernel, grid=grid,
        in_specs=[row_spec(d), row_spec(GDN_QKV), s_spec] + _gdn_weight_specs(wqkv, wc, wab, wg, wout),
        out_specs=[row_spec(d), row_spec(GDN_QKV), s_spec],
        out_shape=[jax.ShapeDtypeStruct(x.shape, F32), jax.ShapeDtypeStruct((rows, GDN_QKV), F32),
                   jax.ShapeDtypeStruct(s0.shape, F32)],
        scratch_shapes=[pltpu.VMEM((GDN_SB, 2 * DEC_SEQ, GDN_QKV), F32),
                        tile(hv), tile(hv), tile(hv), tile(AB_PAD), tile(hv), tile(hv),
                        tile(AB_PAD), tile(GDN_DV), tile(GDN_DV), tile(GDN_DK), tile(GDN_DV), tile(GDN_DV)],
        compiler_params=_params(1), name="gdn_sample")(x, p, s0, g, wqkv, wc, wab, alog, dt, wg, og, wout)


def _to_tm(a, bb=SAMPLE_BB):
    b, t, c = a.shape
    return a.reshape(b // bb, bb, t, c).transpose(0, 2, 1, 3).reshape(b // bb, t * bb, c)


def _from_tm(a, t, bb=SAMPLE_BB):
    nb, _, c = a.shape
    return a.reshape(nb, t, bb, c).transpose(0, 2, 1, 3).reshape(nb * bb, t, c)


def _row(v):
    return v.reshape(1, -1)


def kernel(x_prompt, x_sample, state_pool, state_conv, state_gdn_conv, state_gdn_S, state_ffn_conv,
           norm_mix_g, norm_ffn_g, norm_final_g,
           a_w_in, a_b_in, a_ln_g, a_ln_b, a_w_s, a_b_s, a_w_out,
           b_w_grp, b_scale,
           c_w_pw1, c_b_pw1, c_w_dw, c_b_dw, c_ln_g, c_ln_b, c_w_pw2, c_b_pw2,
           d_w_qkv, d_w_conv, d_w_ab, d_dt_bias, d_a_log, d_w_g, d_o_norm_g, d_w_out,
           f_w_up, f_w_dw, f_b_dw, f_w_down):
    assert DEPTH == 4 and x_prompt.shape == (BATCH, SEQ, D_MODEL) and x_sample.shape == (DEC_BATCH, DEC_SEQ, D_MODEL)
    d = D_MODEL
    xp = x_prompt
    xs = _to_tm(x_sample)

    def ffn(i, xp, xs):
        wup = f_w_up[i].astype(BF16).reshape(d, 2 * FFN_NFC, FFN_FC).transpose(1, 0, 2)
        wdw = f_w_dw[i].reshape(FFN_CONV, FFN_NFC, FFN_FC).transpose(1, 0, 2)
        bdw = f_b_dw[i].reshape(FFN_NFC, 1, FFN_FC)
        wdown = f_w_down[i].astype(BF16).reshape(FFN_NFC, FFN_FC, d)
        g = _row(norm_ffn_g[i])
        fg = _row(norm_final_g) if i == DEPTH - 1 else None
        st = _to_tm(state_ffn_conv[i]).reshape(SAMPLE_NB, FFN_BUF * SAMPLE_BB, FFN_NFC, FFN_FC)
        st = st.transpose(0, 2, 1, 3)
        outs_p = _ffn_call(xp, None, g, wup, wdw, bdw, wdown, fg, sample=False)
        outs_s = _ffn_call(xs, st, g, wup, wdw, bdw, wdown, fg, sample=True)
        fp = outs_p[1][:, :, SUBLANES - FFN_BUF:, :].transpose(0, 2, 1, 3).reshape(BATCH, FFN_BUF, D_FF)
        fs = outs_s[1].transpose(0, 2, 1, 3).reshape(SAMPLE_NB, FFN_BUF * SAMPLE_BB, D_FF)
        fs = _from_tm(fs, FFN_BUF)
        return outs_p, outs_s, fp, fs

    ffn_p, ffn_s = [], []

    g = _row(norm_mix_g[0])
    win, bin_ = a_w_in[0].astype(BF16), _row(a_b_in[0])
    lng, lnb = _row(a_ln_g[0]), _row(a_ln_b[0])
    wout = a_w_out[0].astype(BF16)
    bsx = jnp.repeat(a_b_s[0].T, A_GDIM, axis=1)
    wsx = jnp.repeat(a_w_s[0][:, :DEC_SEQ, :DEC_SEQ].transpose(1, 2, 0), A_GDIM, axis=2)
    wsx = wsx.reshape(DEC_SEQ * DEC_SEQ, d)
    xp, p_chunk_v = _mixa_prompt_call(xp, g, win, bin_, lng, lnb, a_w_s[0], bsx, wout)
    xs, s_v = _mixa_sample_call(xs, g, win, bin_, lng, lnb, wsx, bsx, wout)
    s_chunk_v = _from_tm(s_v, DEC_SEQ)
    op, os_, fp, fs = ffn(0, xp, xs)
    xp, xs = op[0], os_[0]
    ffn_p.append(fp); ffn_s.append(fs)

    g = _row(norm_mix_g[1])
    wg, sc = b_w_grp[0].astype(BF16), _row(b_scale[0])
    xp, pool_p = _pool_call(xp, None, g, wg, sc, sample=False)
    xs, pool_s = _pool_call(xs, _to_tm(state_pool[0]), g, wg, sc, sample=True)
    p_pool = pool_p[:, _round_up(POOL_BUF, SUBLANES) - POOL_BUF:, :]
    s_pool = _from_tm(pool_s, POOL_BUF)
    op, os_, fp, fs = ffn(1, xp, xs)
    xp, xs = op[0], os_[0]
    ffn_p.append(fp); ffn_s.append(fs)

    g = _row(norm_mix_g[2])
    cargs = (g, c_w_pw1[0].astype(BF16), _row(c_b_pw1[0]), c_w_dw[0], _row(c_b_dw[0]), _row(c_ln_g[0]),
             _row(c_ln_b[0]), c_w_pw2[0].astype(BF16), _row(c_b_pw2[0]))
    xp, conv_p = _conf_call(xp, None, *cargs, sample=False)
    xs, conv_s = _conf_call(xs, _to_tm(state_conv[0]), *cargs, sample=True)
    p_conv = conv_p[:, _round_up(CONV_BUF, SUBLANES) - CONV_BUF:, :]
    s_conv = _from_tm(conv_s, CONV_BUF)
    op, os_, fp, fs = ffn(2, xp, xs)
    xp, xs = op[0], os_[0]
    ffn_p.append(fp); ffn_s.append(fs)

    g = _row(norm_mix_g[3])
    wab = jnp.pad(d_w_ab[0], ((0, 0), (0, AB_PAD - 2 * GDN_HEADS))).astype(BF16)
    lane_pad = (GDN_HEADS, AB_PAD - 2 * GDN_HEADS)
    alog = _row(jnp.pad(d_a_log[0], lane_pad))
    dtb = _row(jnp.pad(d_dt_bias[0], lane_pad))
    dargs = (g, d_w_qkv[0].astype(BF16), d_w_conv[0], wab, alog, dtb, d_w_g[0].astype(BF16),
             _row(d_o_norm_g[0]), d_w_out[0].astype(BF16))
    xp, gconv_p, p_gdn_s = _gdn_prompt_call(xp, *dargs)
    xs_bm = _from_tm(xs, DEC_SEQ).reshape(DEC_BATCH * DEC_SEQ, d)
    hist = jnp.pad(state_gdn_conv[0], ((0, 0), (DEC_SEQ - GDN_BUF, 0), (0, 0))).reshape(DEC_BATCH * DEC_SEQ, GDN_QKV)
    xs_bm, pre_s, s_gdn_s = _gdn_sample_call(xs_bm, hist, state_gdn_S[0], *dargs)
    xs = _to_tm(xs_bm.reshape(DEC_BATCH, DEC_SEQ, d))
    p_gdn_conv = gconv_p[:, SUBLANES - GDN_BUF:, :]
    s_gdn_conv = pre_s.reshape(DEC_BATCH, DEC_SEQ, GDN_QKV)[:, DEC_SEQ - GDN_BUF:, :]
    op, os_, fp, fs = ffn(3, xp, xs)
    ffn_p.append(fp); ffn_s.append(fs)
    y_prompt = op[2]
    y_sample = _from_tm(os_[2], DEC_SEQ)

    return (y_prompt, y_sample, p_chunk_v[None], s_chunk_v[None], p_pool[None], s_pool[None],
            p_conv[None], s_conv[None], p_gdn_conv[None], s_gdn_conv[None], p_gdn_s[None], s_gdn_s[None],
            jnp.stack(ffn_p), jnp.stack(ffn_s))
```

```python
import functools

import numpy as np
import jax
import jax.numpy as jnp
from jax import lax
from jax.experimental import pallas as pl
from jax.experimental.pallas import tpu as pltpu

F32 = jnp.float32
BF16 = jnp.bfloat16

D_MODEL = 1024
BATCH = 8
SEQ = 2048
DEPTH = 4
DEC_BATCH = 128
DEC_SEQ = 8
PAST_LEN = 16384
EPS = 1e-6
CHUNK = 128
A_GROUPS = 8
A_GDIM = D_MODEL // A_GROUPS
POOL_WINDOWS = (2, 4, 8, 16)
POOL_GDIM = D_MODEL // len(POOL_WINDOWS)
POOL_BUF = max(POOL_WINDOWS) - 1
CONV_WIDTH = 31
CONV_BUF = CONV_WIDTH - 1
GDN_HEADS = 8
GDN_DK = 128
GDN_DV = 128
GDN_CONV = 4
GDN_BUF = GDN_CONV - 1
GDN_QKV = GDN_HEADS * (2 * GDN_DK + GDN_DV)
D_FF = 2816
FFN_CONV = 3
FFN_BUF = FFN_CONV - 1

SUBLANES = 8
LANES = 128
VMEM_LIMIT_BYTES = 60 * 1024 * 1024

FFN_FC = 256
FFN_NFC = D_FF // FFN_FC
SAMPLE_BB = 32
SAMPLE_M = DEC_SEQ * SAMPLE_BB
SAMPLE_NB = DEC_BATCH // SAMPLE_BB
GDN_C = 64
GDN_PROMPT_M = 256
GDN_R = 128
GDN_SB = GDN_R // DEC_SEQ
AB_PAD = LANES

assert PAST_LEN % CHUNK == 0 and DEC_SEQ <= CHUNK and SEQ % CHUNK == 0 and SEQ % GDN_PROMPT_M == 0


def _n_fused(nilpotency):
    levels = nilpotency.bit_length() - 1
    assert 1 << levels == nilpotency and levels >= 2
    return levels - 2


def _round_up(n, m):
    return (n + m - 1) // m * m


def _rms(x, g):
    return x * lax.rsqrt(jnp.mean(x * x, axis=-1, keepdims=True) + EPS) * g


def _ln(x, g, b):
    mu = jnp.mean(x, axis=-1, keepdims=True)
    xc = x - mu
    var = jnp.mean(xc * xc, axis=-1, keepdims=True)
    return xc * lax.rsqrt(var + EPS) * g + b


def _sigmoid(x):
    return 1.0 / (1.0 + jnp.exp(-x))


def _silu(x):
    return x * _sigmoid(x)


def _gelu_tanh(x):
    c = float(np.sqrt(2.0 / np.pi))
    return x * (0.5 * (1.0 + jnp.tanh(c * (x + 0.044715 * (x * x * x)))))


def _softplus(x):
    return jnp.maximum(x, 0.0) + jnp.log1p(jnp.exp(-jnp.abs(x)))


def _mm(a, b):
    return jnp.dot(a, b, preferred_element_type=F32)


def _dot1(a, b):
    return _mm(a.astype(BF16), b.astype(BF16))


def _dot_nt(a, b):
    return lax.dot_general(a.astype(BF16), b.astype(BF16), (((1,), (1,)), ((), ())),
                           preferred_element_type=F32)


def _dot_tn(a, b):
    return lax.dot_general(a.astype(BF16), b.astype(BF16), (((0,), (0,)), ((), ())),
                           preferred_element_type=F32)


def _split2(a):
    hi = a.astype(BF16)
    lo = (a - hi.astype(F32)).astype(BF16)
    return hi, lo


def _rows(ref, start, rows, cs=slice(None)):
    base = start // SUBLANES * SUBLANES
    delta = start - base
    if delta == 0:
        return ref[start:start + rows, cs]
    end = _round_up(start + rows, SUBLANES)
    return pltpu.roll(ref[base:end, cs], (end - base) - delta, 0)[0:rows]


def _cumsum_rows(tri_bf16, g):
    hi = g.astype(BF16)
    r = g - hi.astype(F32)
    mid = r.astype(BF16)
    lo = (r - mid.astype(F32)).astype(BF16)
    return _mm(tri_bf16, hi) + (_mm(tri_bf16, mid) + _mm(tri_bf16, lo))


def _resident(shape):
    nd = len(shape)
    return pl.BlockSpec(shape, lambda *_: (0,) * nd, pipeline_mode=pl.Buffered(1))


def _params(n_axes):
    return pltpu.CompilerParams(dimension_semantics=("arbitrary",) * n_axes,
                                vmem_limit_bytes=VMEM_LIMIT_BYTES)


def _prompt_tile_spec(m, c):
    return pl.BlockSpec((None, m, c), lambda b, t: (b, t, 0))


def _prompt_seq_spec(r, c):
    return pl.BlockSpec((None, r, c), lambda b, t: (b, 0, 0))


def _sample_spec(r, c):
    return pl.BlockSpec((None, r, c), lambda j: (j, 0, 0))


def _ffn_kernel(*refs, sample, last, m, st, hp):
    it = iter(refs)
    x_ref = next(it)
    st_ref = next(it) if sample else None
    g_ref, wup_ref, wdw_ref, bdw_ref, wdown_ref = (next(it) for _ in range(5))
    fg_ref = next(it) if last else None
    xo_ref = next(it)
    sto_ref = next(it)
    yf_ref = next(it) if last else None
    h_ref, acc_ref, abuf = next(it), next(it), next(it)
    halo = None if sample else next(it)

    x = x_ref[...]
    h_ref[...] = _rms(x, g_ref[...]).astype(BF16)
    acc_ref[...] = x
    if not sample:
        @pl.when(pl.program_id(1) == 0)
        def _():
            halo[...] = jnp.zeros_like(halo)

    def body(f, carry):
        h = h_ref[...]
        a = _mm(h, wup_ref[f])
        b = _mm(h, wup_ref[f + FFN_NFC])
        abuf[0:hp, :] = st_ref[f] if sample else halo[f]
        abuf[hp:hp + m, :] = a
        w = wdw_ref[f]
        conv = w[0:1] * _rows(abuf, hp - 2 * st, m) + w[1:2] * _rows(abuf, hp - st, m) + w[2:3] * a
        new_halo = abuf[m:m + hp, :]
        sto_ref[f] = new_halo
        if not sample:
            halo[f] = new_halo
        y = _silu(conv + bdw_ref[f]) * b
        acc_ref[...] += _mm(y.astype(BF16), wdown_ref[f])
        return carry

    lax.fori_loop(0, FFN_NFC, body, 0, unroll=True)
    out = acc_ref[...]
    xo_ref[...] = out
    if last:
        yf_ref[...] = _rms(out, fg_ref[...])


def _ffn_call(x, state, g, wup, wdw, bdw, wdown, final_g, *, sample):
    last = final_g is not None
    d = D_MODEL
    if sample:
        nb, m, _ = x.shape
        st, hp = SAMPLE_BB, FFN_BUF * SAMPLE_BB
        grid = (nb,)
        x_spec = _sample_spec(m, d)
        st_spec = pl.BlockSpec((None, FFN_NFC, hp, FFN_FC), lambda j: (j, 0, 0, 0))
        st_shape = (nb, FFN_NFC, hp, FFN_FC)
    else:
        bsz, t, _ = x.shape
        m, st, hp = 512, 1, SUBLANES
        grid = (bsz, t // m)
        x_spec = _prompt_tile_spec(m, d)
        st_spec = pl.BlockSpec((None, FFN_NFC, hp, FFN_FC), lambda b, t: (b, 0, 0, 0))
        st_shape = (bsz, FFN_NFC, hp, FFN_FC)
    in_specs = [x_spec] + ([st_spec] if sample else []) + [
        _resident((1, d)), _resident(wup.shape), _resident(wdw.shape), _resident(bdw.shape),
        _resident(wdown.shape)] + ([_resident((1, d))] if last else [])
    args = [x] + ([state] if sample else []) + [g, wup, wdw, bdw, wdown] + ([final_g] if last else [])
    out_specs = [x_spec, st_spec] + ([x_spec] if last else [])
    out_shape = [jax.ShapeDtypeStruct(x.shape, F32), jax.ShapeDtypeStruct(st_shape, F32)] + (
        [jax.ShapeDtypeStruct(x.shape, F32)] if last else [])
    scratch = [pltpu.VMEM((m, d), BF16), pltpu.VMEM((m, d), F32), pltpu.VMEM((hp + m, FFN_FC), F32)]
    if not sample:
        scratch.append(pltpu.VMEM((FFN_NFC, hp, FFN_FC), F32))
    return pl.pallas_call(
        functools.partial(_ffn_kernel, sample=sample, last=last, m=m, st=st, hp=hp),
        grid=grid, in_specs=in_specs, out_specs=out_specs, out_shape=out_shape,
        scratch_shapes=scratch, compiler_params=_params(len(grid)),
        name="ffn_sample" if sample else "ffn_prompt")(*args)


def _mixa_prompt_kernel(x_ref, g_ref, win_ref, bin_ref, lng_ref, lnb_ref, ws_ref, bsx_ref, wout_ref,
                        xo_ref, vo_ref, h_ref, vb_ref, z_ref, *, m):
    d = D_MODEL
    x = x_ref[...]
    h_ref[...] = _rms(x, g_ref[...]).astype(BF16)
    vv = _gelu_tanh(_mm(h_ref[...], win_ref[:, d:2 * d]) + bin_ref[:, d:2 * d])
    v = _ln(vv, lng_ref[...], lnb_ref[...])

    @pl.when(pl.program_id(1) == pl.num_programs(1) - 1)
    def _():
        vo_ref[...] = v[m - CHUNK:m, :]

    vb_ref[...] = v.astype(BF16)
    row = lax.broadcasted_iota(jnp.int32, (CHUNK, CHUNK), 0)
    col = lax.broadcasted_iota(jnp.int32, (CHUNK, CHUNK), 1)
    causal = col <= row
    gpair = 2 * A_GDIM
    for gp in range(A_GROUPS // 2):
        ps = slice(gp * gpair, (gp + 1) * gpair)
        u = _gelu_tanh(_mm(h_ref[...], win_ref[:, ps]) + bin_ref[:, ps])
        for half in range(2):
            grp = 2 * gp + half
            cs = slice(grp * A_GDIM, (grp + 1) * A_GDIM)
            us = slice(half * A_GDIM, (half + 1) * A_GDIM)
            wsg = jnp.where(causal, ws_ref[grp], 0.0).astype(BF16)
            for c in range(m // CHUNK):
                rs = slice(c * CHUNK, (c + 1) * CHUNK)
                mixed = _mm(wsg, vb_ref[rs, cs]) + bsx_ref[:, cs]
                z_ref[rs, cs] = (u[rs, us] * mixed).astype(BF16)
    xo_ref[...] = x + _mm(z_ref[...], wout_ref[...])


def _mixa_sample_kernel(x_ref, g_ref, win_ref, bin_ref, lng_ref, lnb_ref, wsx_ref, bsx_ref, wout_ref,
                        xo_ref, vo_ref, z_ref, *, bb):
    d = D_MODEL
    x = x_ref[...]
    h = _rms(x, g_ref[...]).astype(BF16)
    u = _gelu_tanh(_mm(h, win_ref[:, 0:d]) + bin_ref[:, 0:d])
    vv = _gelu_tanh(_mm(h, win_ref[:, d:2 * d]) + bin_ref[:, d:2 * d])
    v = _ln(vv, lng_ref[...], lnb_ref[...])
    vo_ref[...] = v
    for t in range(DEC_SEQ):
        mixed = bsx_ref[t:t + 1, :]
        for j in range(t + 1):
            mixed = mixed + wsx_ref[t * DEC_SEQ + j:t * DEC_SEQ + j + 1, :] * v[j * bb:(j + 1) * bb, :]
        z_ref[t * bb:(t + 1) * bb, :] = (u[t * bb:(t + 1) * bb, :] * mixed).astype(BF16)
    xo_ref[...] = x + _mm(z_ref[...], wout_ref[...])


def _mixa_prompt_call(x, g, win, bin_, lng, lnb, ws, bsx, wout):
    bsz, t, d = x.shape
    m = 512
    grid = (bsz, t // m)
    return pl.pallas_call(
        functools.partial(_mixa_prompt_kernel, m=m), grid=grid,
        in_specs=[_prompt_tile_spec(m, d), _resident((1, d)), _resident(win.shape), _resident(bin_.shape),
                  _resident((1, d)), _resident((1, d)), _resident(ws.shape), _resident(bsx.shape),
                  _resident(wout.shape)],
        out_specs=[_prompt_tile_spec(m, d), _prompt_seq_spec(CHUNK, d)],
        out_shape=[jax.ShapeDtypeStruct(x.shape, F32), jax.ShapeDtypeStruct((bsz, CHUNK, d), F32)],
        scratch_shapes=[pltpu.VMEM((m, d), BF16), pltpu.VMEM((m, d), BF16), pltpu.VMEM((m, d), BF16)],
        compiler_params=_params(2), name="mixa_prompt")(x, g, win, bin_, lng, lnb, ws, bsx, wout)


def _mixa_sample_call(x, g, win, bin_, lng, lnb, wsx, bsx, wout):
    nb, m, d = x.shape
    return pl.pallas_call(
        functools.partial(_mixa_sample_kernel, bb=SAMPLE_BB), grid=(nb,),
        in_specs=[_sample_spec(m, d), _resident((1, d)), _resident(win.shape), _resident(bin_.shape),
                  _resident((1, d)), _resident((1, d)), _resident(wsx.shape), _resident(bsx.shape),
                  _resident(wout.shape)],
        out_specs=[_sample_spec(m, d), _sample_spec(m, d)],
        out_shape=[jax.ShapeDtypeStruct(x.shape, F32), jax.ShapeDtypeStruct(x.shape, F32)],
        scratch_shapes=[pltpu.VMEM((m, d), BF16)],
        compiler_params=_params(1), name="mixa_sample")(x, g, win, bin_, lng, lnb, wsx, bsx, wout)


def _pool_kernel(*refs, sample, m, st, hp):
    it = iter(refs)
    x_ref = next(it)
    st_ref = next(it) if sample else None
    g_ref, wg_ref, sc_ref = next(it), next(it), next(it)
    xo_ref, sto_ref = next(it), next(it)
    hbuf = next(it)
    halo = None if sample else next(it)

    x = x_ref[...]
    h = _rms(x, g_ref[...])
    if sample:
        hbuf[0:hp, :] = st_ref[...]
        pos0 = PAST_LEN
    else:
        @pl.when(pl.program_id(1) == 0)
        def _():
            halo[...] = jnp.zeros_like(halo)
        hbuf[0:hp, :] = halo[...]
        pos0 = pl.program_id(1) * m
    hbuf[hp:hp + m, :] = h
    new_halo = hbuf[m:m + hp, :]
    sto_ref[...] = new_halo
    if not sample:
        halo[...] = new_halo
    pos = pos0 + lax.broadcasted_iota(jnp.int32, (m, POOL_GDIM), 0) // st
    for gi, win in enumerate(POOL_WINDOWS):
        cs = slice(gi * POOL_GDIM, (gi + 1) * POOL_GDIM)
        s = h[:, cs]
        for j in range(1, win):
            s = s + _rows(hbuf, hp - j * st, m, cs)
        cnt = jnp.minimum(pos + 1, win).astype(F32)
        pooled = s / cnt - h[:, cs]
        y = _dot1(pooled, wg_ref[gi]) * sc_ref[:, cs]
        xo_ref[:, cs] = x[:, cs] + y


def _pool_call(x, state, g, wg, sc, *, sample):
    d = D_MODEL
    if sample:
        nb, m, _ = x.shape
        st, hp = SAMPLE_BB, POOL_BUF * SAMPLE_BB
        grid = (nb,)
        x_spec, st_spec = _sample_spec(m, d), _sample_spec(hp, d)
        st_shape = (nb, hp, d)
    else:
        bsz, t, _ = x.shape
        m, st, hp = 512, 1, _round_up(POOL_BUF, SUBLANES)
        grid = (bsz, t // m)
        x_spec, st_spec = _prompt_tile_spec(m, d), _prompt_seq_spec(hp, d)
        st_shape = (bsz, hp, d)
    in_specs = [x_spec] + ([st_spec] if sample else []) + [
        _resident((1, d)), _resident(wg.shape), _resident((1, d))]
    args = [x] + ([state] if sample else []) + [g, wg, sc]
    scratch = [pltpu.VMEM((hp + m, d), F32)] + ([] if sample else [pltpu.VMEM((hp, d), F32)])
    return pl.pallas_call(
        functools.partial(_pool_kernel, sample=sample, m=m, st=st, hp=hp), grid=grid,
        in_specs=in_specs, out_specs=[x_spec, st_spec],
        out_shape=[jax.ShapeDtypeStruct(x.shape, F32), jax.ShapeDtypeStruct(st_shape, F32)],
        scratch_shapes=scratch, compiler_params=_params(len(grid)),
        name="pool_sample" if sample else "pool_prompt")(*args)


CONV_RB = 64


def _conf_kernel(*refs, sample, m, st, hp):
    it = iter(refs)
    x_ref = next(it)
    st_ref = next(it) if sample else None
    (g_ref, w1_ref, b1_ref, wdw_ref, bdw_ref, lng_ref, lnb_ref, w2_ref, b2_ref) = (next(it) for _ in range(9))
    xo_ref, sto_ref = next(it), next(it)
    zbuf, cbuf = next(it), next(it)
    halo = None if sample else next(it)
    d = D_MODEL

    x = x_ref[...]
    h = _rms(x, g_ref[...]).astype(BF16)
    a = _mm(h, w1_ref[:, 0:d]) + b1_ref[:, 0:d]
    gt = _mm(h, w1_ref[:, d:2 * d]) + b1_ref[:, d:2 * d]
    z = a * _sigmoid(gt)
    if sample:
        zbuf[0:hp, :] = st_ref[...]
    else:
        @pl.when(pl.program_id(1) == 0)
        def _():
            halo[...] = jnp.zeros_like(halo)
        zbuf[0:hp, :] = halo[...]
    zbuf[hp:hp + m, :] = z
    new_halo = zbuf[m:m + hp, :]
    sto_ref[...] = new_halo
    if not sample:
        halo[...] = new_halo
    for c0 in range(0, d, LANES):
        cs = slice(c0, c0 + LANES)
        for r0 in range(0, m, CONV_RB):
            first = hp - (CONV_WIDTH - 1) * st + r0
            acc = None
            if st % SUBLANES == 0:
                for k in range(CONV_WIDTH):
                    term = wdw_ref[k:k + 1, cs] * zbuf[first + k * st:first + k * st + CONV_RB, cs]
                    acc = term if acc is None else acc + term
            else:
                assert st == 1
                for s in range(SUBLANES):
                    taps = range(s, CONV_WIDTH, SUBLANES)
                    win = _rows(zbuf, first + s, CONV_RB + (len(taps) - 1) * SUBLANES, cs)
                    for q, k in enumerate(taps):
                        term = wdw_ref[k:k + 1, cs] * win[q * SUBLANES:q * SUBLANES + CONV_RB]
                        acc = term if acc is None else acc + term
            cbuf[r0:r0 + CONV_RB, cs] = acc
    c = _silu(_ln(cbuf[...] + bdw_ref[...], lng_ref[...], lnb_ref[...]))
    xo_ref[...] = x + _mm(c.astype(BF16), w2_ref[...]) + b2_ref[...]


def _conf_call(x, state, g, w1, b1, wdw, bdw, lng, lnb, w2, b2, *, sample):
    d = D_MODEL
    if sample:
        nb, m, _ = x.shape
        st, hp = SAMPLE_BB, CONV_BUF * SAMPLE_BB
        grid = (nb,)
        x_spec, st_spec = _sample_spec(m, d), _sample_spec(hp, d)
        st_shape = (nb, hp, d)
    else:
        bsz, t, _ = x.shape
        m, st, hp = 256, 1, _round_up(CONV_BUF, SUBLANES)
        grid = (bsz, t // m)
        x_spec, st_spec = _prompt_tile_spec(m, d), _prompt_seq_spec(hp, d)
        st_shape = (bsz, hp, d)
    in_specs = [x_spec] + ([st_spec] if sample else []) + [
        _resident((1, d)), _resident(w1.shape), _resident(b1.shape), _resident(wdw.shape),
        _resident((1, d)), _resident((1, d)), _resident((1, d)), _resident(w2.shape), _resident((1, d))]
    args = [x] + ([state] if sample else []) + [g, w1, b1, wdw, bdw, lng, lnb, w2, b2]
    scratch = [pltpu.VMEM((hp + m, d), F32), pltpu.VMEM((m, d), F32)] + (
        [] if sample else [pltpu.VMEM((hp, d), F32)])
    return pl.pallas_call(
        functools.partial(_conf_kernel, sample=sample, m=m, st=st, hp=hp), grid=grid,
        in_specs=in_specs, out_specs=[x_spec, st_spec],
        out_shape=[jax.ShapeDtypeStruct(x.shape, F32), jax.ShapeDtypeStruct(st_shape, F32)],
        scratch_shapes=scratch, compiler_params=_params(len(grid)),
        name="conf_sample" if sample else "conf_prompt")(*args)


def _gdn_project(h, conv, wab_ref, alog_ref, dt_ref, wg_ref, q_scr, k_scr, v_scr, bg_scr, gate_scr):
    hk = GDN_HEADS * GDN_DK
    qkv = _silu(conv)
    for hd in range(GDN_HEADS):
        cs = slice(hd * GDN_DK, (hd + 1) * GDN_DK)
        qh = qkv[:, hd * GDN_DK:(hd + 1) * GDN_DK]
        kh = qkv[:, hk + hd * GDN_DK:hk + (hd + 1) * GDN_DK]
        q_scr[:, cs] = qh * lax.rsqrt(jnp.sum(qh * qh, axis=-1, keepdims=True) + EPS) * (GDN_DK ** -0.5)
        k_scr[:, cs] = kh * lax.rsqrt(jnp.sum(kh * kh, axis=-1, keepdims=True) + EPS)
    v_scr[...] = qkv[:, 2 * hk:]
    ab = _mm(h, wab_ref[...])
    lane = lax.broadcasted_iota(jnp.int32, ab.shape, 1)
    beta = _sigmoid(ab)
    gl = -jnp.exp(alog_ref[...]) * _softplus(ab + dt_ref[...])
    bg_scr[...] = jnp.where(lane < GDN_HEADS, beta, gl)
    gate_scr[...] = _silu(_mm(h, wg_ref[...]))


def _sdot(lhs, rhs):
    return _dot1(lhs, rhs)


def _neumann_inverse(a_list, eye_c, expand, n_fused):
    r = a_list[0].shape[0]
    apow = [_sdot(a, expand(a)) for a in a_list]
    p = [eye_c - a for a in a_list]
    for _ in range(n_fused):
        outs = [_sdot(jnp.concatenate([ph, ah], axis=0), expand(ah)) for ph, ah in zip(p, apow)]
        p = [ph + o[:r] for ph, o in zip(p, outs)]
        apow = [o[r:] for o in outs]
    return [ph + _sdot(ph, expand(ah)) for ph, ah in zip(p, apow)]


def _wy_tile(q_scr, k_scr, v_scr, bg, gc, gct, incl, strict, eye_c, expand, fold, n_fused):
    m = bg.shape[0]
    a_c, qk_bd, rhs, qe, gcc = [], [], [], [], []
    for hd in range(GDN_HEADS):
        cs = slice(hd * GDN_DK, (hd + 1) * GDN_DK)
        gcol = GDN_HEADS + hd
        gc_c = gc[:, gcol:gcol + 1]
        gc_r = gct[gcol:gcol + 1, :]
        beta_c = bg[:, hd:hd + 1]
        qh, kh = q_scr[:, cs], k_scr[:, cs]
        dec = jnp.where(incl, jnp.exp(jnp.where(incl, gc_c - gc_r, 0.0)), 0.0)
        kb = kh * beta_c
        gram = _dot_nt(jnp.concatenate([kb, qh], axis=0), kh)
        a_c.append(fold(jnp.where(strict, gram[:m] * dec, 0.0)))
        qk_bd.append(gram[m:] * dec)
        egc = jnp.exp(gc_c)
        rhs.append(jnp.concatenate([v_scr[:, cs] * beta_c, kb * egc], axis=1))
        qe.append(qh * egc)
        gcc.append(gc_c)
    t_c = _neumann_inverse(a_c, eye_c, expand, n_fused)
    sol = [_sdot(expand(t), r) for t, r in zip(t_c, rhs)]
    u = [s[:, :GDN_DV] for s in sol]
    w = [s[:, GDN_DV:] for s in sol]
    return u, w, qk_bd, qe, gcc


def _gdn_finish(x, o_scr, gate_scr, og_ref, wout_ref):
    for hd in range(GDN_HEADS):
        cs = slice(hd * GDN_DV, (hd + 1) * GDN_DV)
        oh = o_scr[:, cs]
        on = oh * lax.rsqrt(jnp.mean(oh * oh, axis=-1, keepdims=True) + EPS) * og_ref[...]
        o_scr[:, cs] = on * gate_scr[:, cs]
    return x + _mm(o_scr[...].astype(BF16), wout_ref[...])


def _gdn_prompt_kernel(x_ref, g_ref, wqkv_ref, wc_ref, wab_ref, alog_ref, dt_ref, wg_ref, og_ref, wout_ref,
                       xo_ref, cst_ref, so_ref,
                       cbuf, halo, s_scr, q_scr, k_scr, v_scr, bg_scr, gate_scr, o_scr, *, m):
    hp = SUBLANES
    r = GDN_C

    @pl.when(pl.program_id(1) == 0)
    def _():
        halo[...] = jnp.zeros_like(halo)
        s_scr[...] = jnp.zeros_like(s_scr)

    x = x_ref[...]
    h = _rms(x, g_ref[...]).astype(BF16)
    pre = _mm(h, wqkv_ref[...])
    cbuf[0:hp, :] = halo[...]
    cbuf[hp:hp + m, :] = pre
    conv = wc_ref[GDN_CONV - 1:GDN_CONV, :] * pre
    for k in range(GDN_CONV - 1):
        off = hp - (GDN_CONV - 1 - k)
        conv = conv + wc_ref[k:k + 1, :] * _rows(cbuf, off, m)
    new_halo = cbuf[m:m + hp, :]
    halo[...] = new_halo
    cst_ref[...] = new_halo
    _gdn_project(h, conv, wab_ref, alog_ref, dt_ref, wg_ref, q_scr, k_scr, v_scr, bg_scr, gate_scr)

    nch = m // r
    row = lax.broadcasted_iota(jnp.int32, (m, m), 0)
    col = lax.broadcasted_iota(jnp.int32, (m, m), 1)
    same = (row // r) == (col // r)
    incl = same & (col <= row)
    strict = same & (col < row)
    tri = jnp.where(incl, 1.0, 0.0).astype(BF16)
    er = lax.broadcasted_iota(jnp.int32, (r, m), 0)
    ec = lax.broadcasted_iota(jnp.int32, (r, m), 1)
    eye_c = jnp.where(ec % r == er, 1.0, 0.0).astype(F32)

    def expand(c):
        return jnp.where(same, jnp.concatenate([c] * nch, axis=0), 0.0)

    def fold(a_bd):
        s = a_bd[0:r]
        for c in range(1, nch):
            s = s + a_bd[c * r:(c + 1) * r]
        return s

    bg = bg_scr[...]
    gc = _cumsum_rows(tri, bg)
    u, w, qk_bd, qe, gcc = _wy_tile(q_scr, k_scr, v_scr, bg, gc, gc.T, incl, strict, eye_c,
                                    expand, fold, _n_fused(r))
    heads = range(GDN_HEADS)
    s_cur = [s_scr[hd] for hd in heads]
    for c in range(nch):
        rs = slice(c * r, (c + 1) * r)
        ws = [_dot1(jnp.concatenate([w[hd][rs], qe[hd][rs]], axis=0), s_cur[hd]) for hd in heads]
        v_new = [u[hd][rs] - ws[hd][0:r] for hd in heads]
        for hd in heads:
            cs = slice(hd * GDN_DV, (hd + 1) * GDN_DV)
            o_scr[rs, cs] = ws[hd][r:2 * r] + _dot1(qk_bd[hd][rs, c * r:(c + 1) * r], v_new[hd])
        for hd in heads:
            cs = slice(hd * GDN_DK, (hd + 1) * GDN_DK)
            g_last = gcc[hd][(c + 1) * r - 1:(c + 1) * r, :]
            kd = k_scr[rs, cs] * jnp.exp(g_last - gcc[hd][rs])
            s_cur[hd] = s_cur[hd] * jnp.exp(g_last) + _dot_tn(kd, v_new[hd])
    for hd in heads:
        s_scr[hd] = s_cur[hd]
    xo_ref[...] = _gdn_finish(x, o_scr, gate_scr, og_ref, wout_ref)
    so_ref[...] = s_scr[...]


def _gdn_sample_kernel(x_ref, p_ref, s_ref, g_ref, wqkv_ref, wc_ref, wab_ref, alog_ref, dt_ref, wg_ref,
                       og_ref, wout_ref,
                       xo_ref, pre_ref, so_ref,
                       q_scr, k_scr, v_scr, bg_scr, gate_scr, o_scr):
    r = GDN_R
    t_len = DEC_SEQ
    x = x_ref[...]
    h = _rms(x, g_ref[...]).astype(BF16)
    pre = _mm(h, wqkv_ref[...])
    pre_ref[...] = pre
    hist = p_ref[...]
    step = lax.broadcasted_iota(jnp.int32, (r, GDN_QKV), 0) % t_len
    conv = wc_ref[GDN_CONV - 1:GDN_CONV, :] * pre
    for s in range(1, GDN_CONV):
        shifted = jnp.where(step >= s, pltpu.roll(pre, s, 0), pltpu.roll(hist, r - t_len + s, 0))
        conv = conv + wc_ref[GDN_CONV - 1 - s:GDN_CONV - s, :] * shifted
    _gdn_project(h, conv, wab_ref, alog_ref, dt_ref, wg_ref, q_scr, k_scr, v_scr, bg_scr, gate_scr)

    row = lax.broadcasted_iota(jnp.int32, (r, r), 0)
    col = lax.broadcasted_iota(jnp.int32, (r, r), 1)
    same = (row // t_len) == (col // t_len)
    incl = same & (col <= row)
    strict = same & (col < row)
    eye = jnp.where(col == row, 1.0, 0.0).astype(F32)
    tri = jnp.where(incl, 1.0, 0.0).astype(BF16)
    bg = bg_scr[...]
    gc = _cumsum_rows(tri, bg)
    same_matrix = lambda c: c
    u, w, qk_bd, qe, gcc = _wy_tile(q_scr, k_scr, v_scr, bg, gc, gc.T, incl, strict, eye,
                                    same_matrix, same_matrix, _n_fused(t_len))
    zeros8 = jnp.zeros((t_len, GDN_DK), F32)
    for hd in range(GDN_HEADS):
        cs = slice(hd * GDN_DK, (hd + 1) * GDN_DK)
        v_new, o_inter = [], []
        for b in range(GDN_SB):
            rows = slice(b * t_len, (b + 1) * t_len)
            ws = _dot1(jnp.concatenate([w[hd][rows], qe[hd][rows]], axis=0), s_ref[b, hd])
            v_new.append(u[hd][rows] - ws[0:t_len])
            o_inter.append(ws[t_len:2 * t_len])
        o_scr[:, cs] = jnp.concatenate(o_inter, axis=0) + _dot1(qk_bd[hd], jnp.concatenate(v_new, axis=0))
        for b in range(GDN_SB):
            rows = slice(b * t_len, (b + 1) * t_len)
            g_last = gcc[hd][(b + 1) * t_len - 1:(b + 1) * t_len, :]
            kd = k_scr[rows, cs] * jnp.exp(g_last - gcc[hd][rows])
            kd16 = jnp.concatenate([kd, zeros8], axis=0)
            vn16 = jnp.concatenate([v_new[b], zeros8], axis=0)
            so_ref[b, hd] = s_ref[b, hd] * jnp.exp(g_last) + _dot_tn(kd16, vn16)
    xo_ref[...] = _gdn_finish(x, o_scr, gate_scr, og_ref, wout_ref)


def _gdn_weight_specs(wqkv, wc, wab, wg, wout):
    d = D_MODEL
    return [_resident((1, d)), _resident(wqkv.shape), _resident(wc.shape), _resident(wab.shape),
            _resident((1, AB_PAD)), _resident((1, AB_PAD)), _resident(wg.shape),
            _resident((1, GDN_DV)), _resident(wout.shape)]


def _gdn_prompt_call(x, g, wqkv, wc, wab, alog, dt, wg, og, wout):
    bsz, t, d = x.shape
    m = GDN_PROMPT_M
    hv = GDN_HEADS * GDN_DV
    grid = (bsz, t // m)
    s_spec = pl.BlockSpec((None, GDN_HEADS, GDN_DK, GDN_DV), lambda b, t: (b, 0, 0, 0))
    return pl.pallas_call(
        functools.partial(_gdn_prompt_kernel, m=m), grid=grid,
        in_specs=[_prompt_tile_spec(m, d)] + _gdn_weight_specs(wqkv, wc, wab, wg, wout),
        out_specs=[_prompt_tile_spec(m, d), _prompt_seq_spec(SUBLANES, GDN_QKV), s_spec],
        out_shape=[jax.ShapeDtypeStruct(x.shape, F32),
                   jax.ShapeDtypeStruct((bsz, SUBLANES, GDN_QKV), F32),
                   jax.ShapeDtypeStruct((bsz, GDN_HEADS, GDN_DK, GDN_DV), F32)],
        scratch_shapes=[pltpu.VMEM((SUBLANES + m, GDN_QKV), F32), pltpu.VMEM((SUBLANES, GDN_QKV), F32),
                        pltpu.VMEM((GDN_HEADS, GDN_DK, GDN_DV), F32),
                        pltpu.VMEM((m, hv), F32), pltpu.VMEM((m, hv), F32), pltpu.VMEM((m, hv), F32),
                        pltpu.VMEM((m, AB_PAD), F32), pltpu.VMEM((m, hv), F32), pltpu.VMEM((m, hv), F32)],
        compiler_params=_params(2), name="gdn_prompt")(x, g, wqkv, wc, wab, alog, dt, wg, og, wout)


def _gdn_sample_call(x, p, s0, g, wqkv, wc, wab, alog, dt, wg, og, wout):
    rows, d = x.shape
    r = GDN_R
    hv = GDN_HEADS * GDN_DV
    grid = (rows // r,)
    row_spec = lambda c: pl.BlockSpec((r, c), lambda j: (j, 0))
    s_spec = pl.BlockSpec((GDN_SB, GDN_HEADS, GDN_DK, GDN_DV), lambda j: (j, 0, 0, 0))
    tile = lambda c: pltpu.VMEM((r, c), F32)
    return pl.pallas_call(
        _gdn_sample_kernel, grid=grid,
        in_specs=[row_spec(d), row_spec(GDN_QKV), s_spec] + _gdn_weight_specs(wqkv, wc, wab, wg, wout),
        out_specs=[row_spec(d), row_spec(GDN_QKV), s_spec],
        out_shape=[jax.ShapeDtypeStruct(x.shape, F32), jax.ShapeDtypeStruct((rows, GDN_QKV), F32),
                   jax.ShapeDtypeStruct(s0.shape, F32)],
        scratch_shapes=[tile(hv), tile(hv), tile(hv), tile(AB_PAD), tile(hv), tile(hv)],
        compiler_params=_params(1), name="gdn_sample")(x, p, s0, g, wqkv, wc, wab, alog, dt, wg, og, wout)


def _to_tm(a, bb=SAMPLE_BB):
    b, t, c = a.shape
    return a.reshape(b // bb, bb, t, c).transpose(0, 2, 1, 3).reshape(b // bb, t * bb, c)


def _from_tm(a, t, bb=SAMPLE_BB):
    nb, _, c = a.shape
    return a.reshape(nb, t, bb, c).transpose(0, 2, 1, 3).reshape(nb * bb, t, c)


def _row(v):
    return v.reshape(1, -1)


def kernel(x_prompt, x_sample, state_pool, state_conv, state_gdn_conv, state_gdn_S, state_ffn_conv,
           norm_mix_g, norm_ffn_g, norm_final_g,
           a_w_in, a_b_in, a_ln_g, a_ln_b, a_w_s, a_b_s, a_w_out,
           b_w_grp, b_scale,
           c_w_pw1, c_b_pw1, c_w_dw, c_b_dw, c_ln_g, c_ln_b, c_w_pw2, c_b_pw2,
           d_w_qkv, d_w_conv, d_w_ab, d_dt_bias, d_a_log, d_w_g, d_o_norm_g, d_w_out,
           f_w_up, f_w_dw, f_b_dw, f_w_down):
    assert DEPTH == 4 and x_prompt.shape == (BATCH, SEQ, D_MODEL) and x_sample.shape == (DEC_BATCH, DEC_SEQ, D_MODEL)
    d = D_MODEL
    xp = x_prompt
    xs = _to_tm(x_sample)

    def ffn(i, xp, xs):
        wup = f_w_up[i].astype(BF16).reshape(d, 2 * FFN_NFC, FFN_FC).transpose(1, 0, 2)
        wdw = f_w_dw[i].reshape(FFN_CONV, FFN_NFC, FFN_FC).transpose(1, 0, 2)
        bdw = f_b_dw[i].reshape(FFN_NFC, 1, FFN_FC)
        wdown = f_w_down[i].astype(BF16).reshape(FFN_NFC, FFN_FC, d)
        g = _row(norm_ffn_g[i])
        fg = _row(norm_final_g) if i == DEPTH - 1 else None
        st = _to_tm(state_ffn_conv[i]).reshape(SAMPLE_NB, FFN_BUF * SAMPLE_BB, FFN_NFC, FFN_FC)
        st = st.transpose(0, 2, 1, 3)
        outs_p = _ffn_call(xp, None, g, wup, wdw, bdw, wdown, fg, sample=False)
        outs_s = _ffn_call(xs, st, g, wup, wdw, bdw, wdown, fg, sample=True)
        fp = outs_p[1][:, :, SUBLANES - FFN_BUF:, :].transpose(0, 2, 1, 3).reshape(BATCH, FFN_BUF, D_FF)
        fs = outs_s[1].transpose(0, 2, 1, 3).reshape(SAMPLE_NB, FFN_BUF * SAMPLE_BB, D_FF)
        fs = _from_tm(fs, FFN_BUF)
        return outs_p, outs_s, fp, fs

    ffn_p, ffn_s = [], []

    g = _row(norm_mix_g[0])
    win, bin_ = a_w_in[0].astype(BF16), _row(a_b_in[0])
    lng, lnb = _row(a_ln_g[0]), _row(a_ln_b[0])
    wout = a_w_out[0].astype(BF16)
    bsx = jnp.repeat(a_b_s[0].T, A_GDIM, axis=1)
    wsx = jnp.repeat(a_w_s[0][:, :DEC_SEQ, :DEC_SEQ].transpose(1, 2, 0), A_GDIM, axis=2)
    wsx = wsx.reshape(DEC_SEQ * DEC_SEQ, d)
    xp, p_chunk_v = _mixa_prompt_call(xp, g, win, bin_, lng, lnb, a_w_s[0], bsx, wout)
    xs, s_v = _mixa_sample_call(xs, g, win, bin_, lng, lnb, wsx, bsx, wout)
    s_chunk_v = _from_tm(s_v, DEC_SEQ)
    op, os_, fp, fs = ffn(0, xp, xs)
    xp, xs = op[0], os_[0]
    ffn_p.append(fp); ffn_s.append(fs)

    g = _row(norm_mix_g[1])
    wg, sc = b_w_grp[0].astype(BF16), _row(b_scale[0])
    xp, pool_p = _pool_call(xp, None, g, wg, sc, sample=False)
    xs, pool_s = _pool_call(xs, _to_tm(state_pool[0]), g, wg, sc, sample=True)
    p_pool = pool_p[:, _round_up(POOL_BUF, SUBLANES) - POOL_BUF:, :]
    s_pool = _from_tm(pool_s, POOL_BUF)
    op, os_, fp, fs = ffn(1, xp, xs)
    xp, xs = op[0], os_[0]
    ffn_p.append(fp); ffn_s.append(fs)

    g = _row(norm_mix_g[2])
    cargs = (g, c_w_pw1[0].astype(BF16), _row(c_b_pw1[0]), c_w_dw[0], _row(c_b_dw[0]), _row(c_ln_g[0]),
             _row(c_ln_b[0]), c_w_pw2[0].astype(BF16), _row(c_b_pw2[0]))
    xp, conv_p = _conf_call(xp, None, *cargs, sample=False)
    xs, conv_s = _conf_call(xs, _to_tm(state_conv[0]), *cargs, sample=True)
    p_conv = conv_p[:, _round_up(CONV_BUF, SUBLANES) - CONV_BUF:, :]
    s_conv = _from_tm(conv_s, CONV_BUF)
    op, os_, fp, fs = ffn(2, xp, xs)
    xp, xs = op[0], os_[0]
    ffn_p.append(fp); ffn_s.append(fs)

    g = _row(norm_mix_g[3])
    wab = jnp.pad(d_w_ab[0], ((0, 0), (0, AB_PAD - 2 * GDN_HEADS))).astype(BF16)
    lane_pad = (GDN_HEADS, AB_PAD - 2 * GDN_HEADS)
    alog = _row(jnp.pad(d_a_log[0], lane_pad))
    dtb = _row(jnp.pad(d_dt_bias[0], lane_pad))
    dargs = (g, d_w_qkv[0].astype(BF16), d_w_conv[0], wab, alog, dtb, d_w_g[0].astype(BF16),
             _row(d_o_norm_g[0]), d_w_out[0].astype(BF16))
    xp, gconv_p, p_gdn_s = _gdn_prompt_call(xp, *dargs)
    xs_bm = _from_tm(xs, DEC_SEQ).reshape(DEC_BATCH * DEC_SEQ, d)
    hist = jnp.pad(state_gdn_conv[0], ((0, 0), (DEC_SEQ - GDN_BUF, 0), (0, 0))).reshape(DEC_BATCH * DEC_SEQ, GDN_QKV)
    xs_bm, pre_s, s_gdn_s = _gdn_sample_call(xs_bm, hist, state_gdn_S[0], *dargs)
    xs = _to_tm(xs_bm.reshape(DEC_BATCH, DEC_SEQ, d))
    p_gdn_conv = gconv_p[:, SUBLANES - GDN_BUF:, :]
    s_gdn_conv = pre_s.reshape(DEC_BATCH, DEC_SEQ, GDN_QKV)[:, DEC_SEQ - GDN_BUF:, :]
    op, os_, fp, fs = ffn(3, xp, xs)
    ffn_p.append(fp); ffn_s.append(fs)
    y_prompt = op[2]
    y_sample = _from_tm(os_[2], DEC_SEQ)

    return (y_prompt, y_sample, p_chunk_v[None], s_chunk_v[None], p_pool[None], s_pool[None],
            p_conv[None], s_conv[None], p_gdn_conv[None], s_gdn_conv[None], p_gdn_s[None], s_gdn_s[None],
            jnp.stack(ffn_p), jnp.stack(ffn_s))
```

```python
import functools

import numpy as np
import jax
import jax.numpy as jnp
from jax import lax
from jax.experimental import pallas as pl
from jax.experimental.pallas import tpu as pltpu

F32 = jnp.float32
BF16 = jnp.bfloat16

D_MODEL = 1024
BATCH = 8
SEQ = 2048
DEPTH = 4
DEC_BATCH = 128
DEC_SEQ = 8
PAST_LEN = 16384
EPS = 1e-6
CHUNK = 128
A_GROUPS = 8
A_GDIM = D_MODEL // A_GROUPS
POOL_WINDOWS = (2, 4, 8, 16)
POOL_GDIM = D_MODEL // len(POOL_WINDOWS)
POOL_BUF = max(POOL_WINDOWS) - 1
CONV_WIDTH = 31
CONV_BUF = CONV_WIDTH - 1
GDN_HEADS = 8
GDN_DK = 128
GDN_DV = 128
GDN_CONV = 4
GDN_BUF = GDN_CONV - 1
GDN_QKV = GDN_HEADS * (2 * GDN_DK + GDN_DV)
D_FF = 2816
FFN_CONV = 3
FFN_BUF = FFN_CONV - 1

SUBLANES = 8
LANES = 128
VMEM_LIMIT_BYTES = 60 * 1024 * 1024

FFN_FC = 256
FFN_NFC = D_FF // FFN_FC
SAMPLE_BB = 32
SAMPLE_M = DEC_SEQ * SAMPLE_BB
SAMPLE_NB = DEC_BATCH // SAMPLE_BB
FFN_PROMPT_M = 1024
MIXA_PROMPT_M = 1024
POOL_PROMPT_M = 512
CONF_PROMPT_M = 512
GDN_C = 64
GDN_PROMPT_M = 256
GDN_R = 128
GDN_SB = GDN_R // DEC_SEQ
AB_PAD = LANES

assert PAST_LEN % CHUNK == 0 and DEC_SEQ <= CHUNK and SEQ % CHUNK == 0 and SEQ % GDN_PROMPT_M == 0


def _n_fused(nilpotency):
    levels = nilpotency.bit_length() - 1
    assert 1 << levels == nilpotency and levels >= 2
    return levels - 2


def _round_up(n, m):
    return (n + m - 1) // m * m


def _rms(x, g):
    return x * lax.rsqrt(jnp.mean(x * x, axis=-1, keepdims=True) + EPS) * g


def _ln(x, g, b):
    mu = jnp.mean(x, axis=-1, keepdims=True)
    xc = x - mu
    var = jnp.mean(xc * xc, axis=-1, keepdims=True)
    return xc * lax.rsqrt(var + EPS) * g + b


def _sigmoid(x):
    return 1.0 / (1.0 + jnp.exp(-x))


def _silu(x):
    return x * _sigmoid(x)


def _gelu_tanh(x):
    c = float(np.sqrt(2.0 / np.pi))
    return x * (0.5 * (1.0 + jnp.tanh(c * (x + 0.044715 * (x * x * x)))))


def _softplus(x):
    return jnp.maximum(x, 0.0) + jnp.log1p(jnp.exp(-jnp.abs(x)))


def _mm(a, b):
    return jnp.dot(a, b, preferred_element_type=F32)


def _dot1(a, b):
    return _mm(a.astype(BF16), b.astype(BF16))


def _dot_nt(a, b):
    return lax.dot_general(a.astype(BF16), b.astype(BF16), (((1,), (1,)), ((), ())),
                           preferred_element_type=F32)


def _dot_tn(a, b):
    return lax.dot_general(a.astype(BF16), b.astype(BF16), (((0,), (0,)), ((), ())),
                           preferred_element_type=F32)


def _split2(a):
    hi = a.astype(BF16)
    lo = (a - hi.astype(F32)).astype(BF16)
    return hi, lo


def _rows(ref, start, rows, cs=slice(None)):
    base = start // SUBLANES * SUBLANES
    delta = start - base
    if delta == 0:
        return ref[start:start + rows, cs]
    end = _round_up(start + rows, SUBLANES)
    return pltpu.roll(ref[base:end, cs], (end - base) - delta, 0)[0:rows]


def _cumsum_rows(tri_bf16, g):
    hi = g.astype(BF16)
    r = g - hi.astype(F32)
    mid = r.astype(BF16)
    lo = (r - mid.astype(F32)).astype(BF16)
    return _mm(tri_bf16, hi) + (_mm(tri_bf16, mid) + _mm(tri_bf16, lo))


def _resident(shape):
    nd = len(shape)
    return pl.BlockSpec(shape, lambda *_: (0,) * nd, pipeline_mode=pl.Buffered(1))


def _params(n_axes):
    return pltpu.CompilerParams(dimension_semantics=("arbitrary",) * n_axes,
                                vmem_limit_bytes=VMEM_LIMIT_BYTES)


def _prompt_tile_spec(m, c):
    return pl.BlockSpec((None, m, c), lambda b, t: (b, t, 0))


def _prompt_seq_spec(r, c):
    return pl.BlockSpec((None, r, c), lambda b, t: (b, 0, 0))


def _sample_spec(t, c):
    return pl.BlockSpec((SAMPLE_BB, t, c), lambda j: (j, 0, 0))


def _load_tm(ref, nt, cs=slice(None)):
    return jnp.concatenate([ref[:, t, cs] for t in range(nt)], axis=0)


def _store_tm(ref, val, nt, cs=slice(None)):
    bb = val.shape[0] // nt
    for t in range(nt):
        ref[:, t, cs] = val[t * bb:(t + 1) * bb]


def _ffn_kernel(*refs, sample, last, m, st, hp):
    it = iter(refs)
    x_ref = next(it)
    st_ref = next(it) if sample else None
    g_ref, wup_ref, wdw_ref, bdw_ref, wdown_ref = (next(it) for _ in range(5))
    fg_ref = next(it) if last else None
    xo_ref = next(it)
    sto_ref = next(it)
    yf_ref = next(it) if last else None
    h_ref, y_ref, abuf = next(it), next(it), next(it)
    halo = None if sample else next(it)

    x = _load_tm(x_ref, DEC_SEQ) if sample else x_ref[...]
    h_ref[...] = _rms(x, g_ref[...]).astype(BF16)
    if not sample:
        @pl.when(pl.program_id(1) == 0)
        def _():
            halo[...] = jnp.zeros_like(halo)

    for f in range(FFN_NFC):
        cs = slice(f * FFN_FC, (f + 1) * FFN_FC)
        vs = slice(D_FF + f * FFN_FC, D_FF + (f + 1) * FFN_FC)
        h = h_ref[...]
        a = _mm(h, wup_ref[:, cs])
        b = _mm(h, wup_ref[:, vs])
        if sample:
            for j in range(FFN_BUF):
                abuf[j * st:(j + 1) * st, :] = st_ref[:, j, cs]
        else:
            abuf[0:hp, :] = halo[:, cs]
        abuf[hp:hp + m, :] = a
        conv = (wdw_ref[0:1, cs] * _rows(abuf, hp - 2 * st, m) + wdw_ref[1:2, cs] * _rows(abuf, hp - st, m)
                + wdw_ref[2:3, cs] * a)
        new_halo = abuf[m:m + hp, :]
        if sample:
            _store_tm(sto_ref, new_halo, FFN_BUF, cs)
        else:
            halo[:, cs] = new_halo
        y_ref[:, cs] = (_silu(conv + bdw_ref[:, cs]) * b).astype(BF16)
    out = x + _mm(y_ref[...], wdown_ref[...])
    if sample:
        _store_tm(xo_ref, out, DEC_SEQ)
    else:
        xo_ref[...] = out
        sto_ref[...] = halo[...]
    if last:
        yf = _rms(out, fg_ref[...])
        if sample:
            _store_tm(yf_ref, yf, DEC_SEQ)
        else:
            yf_ref[...] = yf


def _layer_resident(shape, layer):
    nd = len(shape) - 1
    return pl.BlockSpec((None,) + tuple(shape[1:]), lambda *_: (layer,) + (0,) * nd,
                        pipeline_mode=pl.Buffered(1))


def _ffn_call(x, state, g, wup, wdw, bdw, wdown, final_g, *, layer, sample):
    last = final_g is not None
    d = D_MODEL
    if sample:
        bsz = x.shape[0]
        m, st, hp = SAMPLE_M, SAMPLE_BB, FFN_BUF * SAMPLE_BB
        grid = (bsz // SAMPLE_BB,)
        x_spec = _sample_spec(DEC_SEQ, d)
        st_spec = _sample_spec(FFN_BUF, D_FF)
        st_shape = (bsz, FFN_BUF, D_FF)
    else:
        bsz, t, _ = x.shape
        m, st, hp = FFN_PROMPT_M, 1, SUBLANES
        grid = (bsz, t // m)
        x_spec = _prompt_tile_spec(m, d)
        st_spec = _prompt_seq_spec(hp, D_FF)
        st_shape = (bsz, hp, D_FF)
    in_specs = [x_spec] + ([st_spec] if sample else []) + [
        _resident((1, d)), _layer_resident(wup.shape, layer), _layer_resident(wdw.shape, layer),
        _layer_resident(bdw.shape, layer), _layer_resident(wdown.shape, layer)] + (
        [_resident((1, d))] if last else [])
    args = [x] + ([state] if sample else []) + [g, wup, wdw, bdw, wdown] + ([final_g] if last else [])
    out_specs = [x_spec, st_spec] + ([x_spec] if last else [])
    out_shape = [jax.ShapeDtypeStruct(x.shape, F32), jax.ShapeDtypeStruct(st_shape, F32)] + (
        [jax.ShapeDtypeStruct(x.shape, F32)] if last else [])
    scratch = [pltpu.VMEM((m, d), BF16), pltpu.VMEM((m, D_FF), BF16), pltpu.VMEM((hp + m, FFN_FC), F32)]
    if not sample:
        scratch.append(pltpu.VMEM((hp, D_FF), F32))
    return pl.pallas_call(
        functools.partial(_ffn_kernel, sample=sample, last=last, m=m, st=st, hp=hp),
        grid=grid, in_specs=in_specs, out_specs=out_specs, out_shape=out_shape,
        scratch_shapes=scratch, compiler_params=_params(len(grid)),
        name="ffn_sample" if sample else "ffn_prompt")(*args)


def _mixa_prompt_kernel(x_ref, g_ref, win_ref, bin_ref, lng_ref, lnb_ref, ws_ref, bsx_ref, wout_ref,
                        xo_ref, vo_ref, h_ref, vb_ref, z_ref, *, m):
    d = D_MODEL
    x = x_ref[...]
    h_ref[...] = _rms(x, g_ref[...]).astype(BF16)
    vv = _gelu_tanh(_mm(h_ref[...], win_ref[:, d:2 * d]) + bin_ref[:, d:2 * d])
    v = _ln(vv, lng_ref[...], lnb_ref[...])

    @pl.when(pl.program_id(1) == pl.num_programs(1) - 1)
    def _():
        vo_ref[...] = v[m - CHUNK:m, :]

    vb_ref[...] = v.astype(BF16)
    row = lax.broadcasted_iota(jnp.int32, (CHUNK, CHUNK), 0)
    col = lax.broadcasted_iota(jnp.int32, (CHUNK, CHUNK), 1)
    causal = col <= row
    gpair = 2 * A_GDIM
    for gp in range(A_GROUPS // 2):
        ps = slice(gp * gpair, (gp + 1) * gpair)
        u = _gelu_tanh(_mm(h_ref[...], win_ref[:, ps]) + bin_ref[:, ps])
        for half in range(2):
            grp = 2 * gp + half
            cs = slice(grp * A_GDIM, (grp + 1) * A_GDIM)
            us = slice(half * A_GDIM, (half + 1) * A_GDIM)
            wsg = jnp.where(causal, ws_ref[grp], 0.0).astype(BF16)
            for c in range(m // CHUNK):
                rs = slice(c * CHUNK, (c + 1) * CHUNK)
                mixed = _mm(wsg, vb_ref[rs, cs]) + bsx_ref[:, cs]
                z_ref[rs, cs] = (u[rs, us] * mixed).astype(BF16)
    xo_ref[...] = x + _mm(z_ref[...], wout_ref[...])


def _mixa_sample_kernel(x_ref, g_ref, win_ref, bin_ref, lng_ref, lnb_ref, wsx_ref, bsx_ref, wout_ref,
                        xo_ref, vo_ref, z_ref, *, bb):
    d = D_MODEL
    x = _load_tm(x_ref, DEC_SEQ)
    h = _rms(x, g_ref[...]).astype(BF16)
    u = _gelu_tanh(_mm(h, win_ref[:, 0:d]) + bin_ref[:, 0:d])
    vv = _gelu_tanh(_mm(h, win_ref[:, d:2 * d]) + bin_ref[:, d:2 * d])
    v = _ln(vv, lng_ref[...], lnb_ref[...])
    _store_tm(vo_ref, v, DEC_SEQ)
    for t in range(DEC_SEQ):
        mixed = bsx_ref[t:t + 1, :]
        for j in range(t + 1):
            mixed = mixed + wsx_ref[t * DEC_SEQ + j:t * DEC_SEQ + j + 1, :] * v[j * bb:(j + 1) * bb, :]
        z_ref[t * bb:(t + 1) * bb, :] = (u[t * bb:(t + 1) * bb, :] * mixed).astype(BF16)
    _store_tm(xo_ref, x + _mm(z_ref[...], wout_ref[...]), DEC_SEQ)


def _mixa_prompt_call(x, g, win, bin_, lng, lnb, ws, bsx, wout):
    bsz, t, d = x.shape
    m = MIXA_PROMPT_M
    grid = (bsz, t // m)
    return pl.pallas_call(
        functools.partial(_mixa_prompt_kernel, m=m), grid=grid,
        in_specs=[_prompt_tile_spec(m, d), _resident((1, d)), _resident(win.shape), _resident(bin_.shape),
                  _resident((1, d)), _resident((1, d)), _resident(ws.shape), _resident(bsx.shape),
                  _resident(wout.shape)],
        out_specs=[_prompt_tile_spec(m, d), _prompt_seq_spec(CHUNK, d)],
        out_shape=[jax.ShapeDtypeStruct(x.shape, F32), jax.ShapeDtypeStruct((bsz, CHUNK, d), F32)],
        scratch_shapes=[pltpu.VMEM((m, d), BF16), pltpu.VMEM((m, d), BF16), pltpu.VMEM((m, d), BF16)],
        compiler_params=_params(2), name="mixa_prompt")(x, g, win, bin_, lng, lnb, ws, bsx, wout)


def _mixa_sample_call(x, g, win, bin_, lng, lnb, wsx, bsx, wout):
    bsz, t, d = x.shape
    return pl.pallas_call(
        functools.partial(_mixa_sample_kernel, bb=SAMPLE_BB), grid=(bsz // SAMPLE_BB,),
        in_specs=[_sample_spec(t, d), _resident((1, d)), _resident(win.shape), _resident(bin_.shape),
                  _resident((1, d)), _resident((1, d)), _resident(wsx.shape), _resident(bsx.shape),
                  _resident(wout.shape)],
        out_specs=[_sample_spec(t, d), _sample_spec(t, d)],
        out_shape=[jax.ShapeDtypeStruct(x.shape, F32), jax.ShapeDtypeStruct(x.shape, F32)],
        scratch_shapes=[pltpu.VMEM((SAMPLE_M, d), BF16)],
        compiler_params=_params(1), name="mixa_sample")(x, g, win, bin_, lng, lnb, wsx, bsx, wout)


def _pool_kernel(*refs, sample, m, st, hp):
    it = iter(refs)
    x_ref = next(it)
    st_ref = next(it) if sample else None
    g_ref, wg_ref, sc_ref = next(it), next(it), next(it)
    xo_ref, sto_ref = next(it), next(it)
    hbuf = next(it)
    halo = None if sample else next(it)

    x = _load_tm(x_ref, DEC_SEQ) if sample else x_ref[...]
    h = _rms(x, g_ref[...])
    if sample:
        for j in range(POOL_BUF):
            hbuf[j * st:(j + 1) * st, :] = st_ref[:, j, :]
        pos0 = PAST_LEN
    else:
        @pl.when(pl.program_id(1) == 0)
        def _():
            halo[...] = jnp.zeros_like(halo)
        hbuf[0:hp, :] = halo[...]
        pos0 = pl.program_id(1) * m
    hbuf[hp:hp + m, :] = h
    new_halo = hbuf[m:m + hp, :]
    if sample:
        _store_tm(sto_ref, new_halo, POOL_BUF)
    else:
        sto_ref[...] = new_halo
        halo[...] = new_halo
    pos = pos0 + lax.broadcasted_iota(jnp.int32, (m, POOL_GDIM), 0) // st
    outs = []
    for gi, win in enumerate(POOL_WINDOWS):
        cs = slice(gi * POOL_GDIM, (gi + 1) * POOL_GDIM)
        s = h[:, cs]
        for j in range(1, win):
            s = s + _rows(hbuf, hp - j * st, m, cs)
        cnt = jnp.minimum(pos + 1, win).astype(F32)
        pooled = s / cnt - h[:, cs]
        outs.append(x[:, cs] + _dot1(pooled, wg_ref[gi]) * sc_ref[:, cs])
    if sample:
        for gi, o in enumerate(outs):
            _store_tm(xo_ref, o, DEC_SEQ, slice(gi * POOL_GDIM, (gi + 1) * POOL_GDIM))
    else:
        for gi, o in enumerate(outs):
            xo_ref[:, gi * POOL_GDIM:(gi + 1) * POOL_GDIM] = o


def _pool_call(x, state, g, wg, sc, *, sample):
    d = D_MODEL
    if sample:
        bsz = x.shape[0]
        m, st, hp = SAMPLE_M, SAMPLE_BB, POOL_BUF * SAMPLE_BB
        grid = (bsz // SAMPLE_BB,)
        x_spec, st_spec = _sample_spec(DEC_SEQ, d), _sample_spec(POOL_BUF, d)
        st_shape = (bsz, POOL_BUF, d)
    else:
        bsz, t, _ = x.shape
        m, st, hp = POOL_PROMPT_M, 1, _round_up(POOL_BUF, SUBLANES)
        grid = (bsz, t // m)
        x_spec, st_spec = _prompt_tile_spec(m, d), _prompt_seq_spec(hp, d)
        st_shape = (bsz, hp, d)
    in_specs = [x_spec] + ([st_spec] if sample else []) + [
        _resident((1, d)), _resident(wg.shape), _resident((1, d))]
    args = [x] + ([state] if sample else []) + [g, wg, sc]
    scratch = [pltpu.VMEM((hp + m, d), F32)] + ([] if sample else [pltpu.VMEM((hp, d), F32)])
    return pl.pallas_call(
        functools.partial(_pool_kernel, sample=sample, m=m, st=st, hp=hp), grid=grid,
        in_specs=in_specs, out_specs=[x_spec, st_spec],
        out_shape=[jax.ShapeDtypeStruct(x.shape, F32), jax.ShapeDtypeStruct(st_shape, F32)],
        scratch_shapes=scratch, compiler_params=_params(len(grid)),
        name="pool_sample" if sample else "pool_prompt")(*args)


CONV_RB = 64


def _conf_kernel(*refs, sample, m, st, hp):
    it = iter(refs)
    x_ref = next(it)
    st_ref = next(it) if sample else None
    (g_ref, w1_ref, b1_ref, wdw_ref, bdw_ref, lng_ref, lnb_ref, w2_ref, b2_ref) = (next(it) for _ in range(9))
    xo_ref, sto_ref = next(it), next(it)
    zbuf, cbuf = next(it), next(it)
    halo = None if sample else next(it)
    d = D_MODEL

    x = _load_tm(x_ref, DEC_SEQ) if sample else x_ref[...]
    h = _rms(x, g_ref[...]).astype(BF16)
    a = _mm(h, w1_ref[:, 0:d]) + b1_ref[:, 0:d]
    gt = _mm(h, w1_ref[:, d:2 * d]) + b1_ref[:, d:2 * d]
    z = a * _sigmoid(gt)
    if sample:
        for j in range(CONV_BUF):
            zbuf[j * st:(j + 1) * st, :] = st_ref[:, j, :]
    else:
        @pl.when(pl.program_id(1) == 0)
        def _():
            halo[...] = jnp.zeros_like(halo)
        zbuf[0:hp, :] = halo[...]
    zbuf[hp:hp + m, :] = z
    new_halo = zbuf[m:m + hp, :]
    if sample:
        _store_tm(sto_ref, new_halo, CONV_BUF)
    else:
        sto_ref[...] = new_halo
        halo[...] = new_halo
    for c0 in range(0, d, LANES):
        cs = slice(c0, c0 + LANES)
        for r0 in range(0, m, CONV_RB):
            first = hp - (CONV_WIDTH - 1) * st + r0
            acc = None
            if st % SUBLANES == 0:
                for k in range(CONV_WIDTH):
                    term = wdw_ref[k:k + 1, cs] * zbuf[first + k * st:first + k * st + CONV_RB, cs]
                    acc = term if acc is None else acc + term
            else:
                assert st == 1
                for s in range(SUBLANES):
                    taps = range(s, CONV_WIDTH, SUBLANES)
                    win = _rows(zbuf, first + s, CONV_RB + (len(taps) - 1) * SUBLANES, cs)
                    for q, k in enumerate(taps):
                        term = wdw_ref[k:k + 1, cs] * win[q * SUBLANES:q * SUBLANES + CONV_RB]
                        acc = term if acc is None else acc + term
            cbuf[r0:r0 + CONV_RB, cs] = acc
    c = _silu(_ln(cbuf[...] + bdw_ref[...], lng_ref[...], lnb_ref[...]))
    out = x + _mm(c.astype(BF16), w2_ref[...]) + b2_ref[...]
    if sample:
        _store_tm(xo_ref, out, DEC_SEQ)
    else:
        xo_ref[...] = out


def _conf_call(x, state, g, w1, b1, wdw, bdw, lng, lnb, w2, b2, *, sample):
    d = D_MODEL
    if sample:
        bsz = x.shape[0]
        m, st, hp = SAMPLE_M, SAMPLE_BB, CONV_BUF * SAMPLE_BB
        grid = (bsz // SAMPLE_BB,)
        x_spec, st_spec = _sample_spec(DEC_SEQ, d), _sample_spec(CONV_BUF, d)
        st_shape = (bsz, CONV_BUF, d)
    else:
        bsz, t, _ = x.shape
        m, st, hp = CONF_PROMPT_M, 1, _round_up(CONV_BUF, SUBLANES)
        grid = (bsz, t // m)
        x_spec, st_spec = _prompt_tile_spec(m, d), _prompt_seq_spec(hp, d)
        st_shape = (bsz, hp, d)
    in_specs = [x_spec] + ([st_spec] if sample else []) + [
        _resident((1, d)), _resident(w1.shape), _resident(b1.shape), _resident(wdw.shape),
        _resident((1, d)), _resident((1, d)), _resident((1, d)), _resident(w2.shape), _resident((1, d))]
    args = [x] + ([state] if sample else []) + [g, w1, b1, wdw, bdw, lng, lnb, w2, b2]
    scratch = [pltpu.VMEM((hp + m, d), F32), pltpu.VMEM((m, d), F32)] + (
        [] if sample else [pltpu.VMEM((hp, d), F32)])
    return pl.pallas_call(
        functools.partial(_conf_kernel, sample=sample, m=m, st=st, hp=hp), grid=grid,
        in_specs=in_specs, out_specs=[x_spec, st_spec],
        out_shape=[jax.ShapeDtypeStruct(x.shape, F32), jax.ShapeDtypeStruct(st_shape, F32)],
        scratch_shapes=scratch, compiler_params=_params(len(grid)),
        name="conf_sample" if sample else "conf_prompt")(*args)


def _gdn_project(h, conv, wab_ref, alog_ref, dt_ref, wg_ref, q_scr, k_scr, v_scr, bg_scr, gate_scr):
    hk = GDN_HEADS * GDN_DK
    qkv = _silu(conv)
    for hd in range(GDN_HEADS):
        cs = slice(hd * GDN_DK, (hd + 1) * GDN_DK)
        qh = qkv[:, hd * GDN_DK:(hd + 1) * GDN_DK]
        kh = qkv[:, hk + hd * GDN_DK:hk + (hd + 1) * GDN_DK]
        q_scr[:, cs] = qh * lax.rsqrt(jnp.sum(qh * qh, axis=-1, keepdims=True) + EPS) * (GDN_DK ** -0.5)
        k_scr[:, cs] = kh * lax.rsqrt(jnp.sum(kh * kh, axis=-1, keepdims=True) + EPS)
    v_scr[...] = qkv[:, 2 * hk:]
    ab = _mm(h, wab_ref[...])
    lane = lax.broadcasted_iota(jnp.int32, ab.shape, 1)
    beta = _sigmoid(ab)
    gl = -jnp.exp(alog_ref[...]) * _softplus(ab + dt_ref[...])
    bg_scr[...] = jnp.where(lane < GDN_HEADS, beta, gl)
    gate_scr[...] = _silu(_mm(h, wg_ref[...]))


def _sdot(lhs, rhs):
    return _dot1(lhs, rhs)


def _neumann_inverse(a_list, eye_c, expand, n_fused):
    r = a_list[0].shape[0]
    apow = [_sdot(a, expand(a)) for a in a_list]
    p = [eye_c - a for a in a_list]
    for _ in range(n_fused):
        outs = [_sdot(jnp.concatenate([ph, ah], axis=0), expand(ah)) for ph, ah in zip(p, apow)]
        p = [ph + o[:r] for ph, o in zip(p, outs)]
        apow = [o[r:] for o in outs]
    return [ph + _sdot(ph, expand(ah)) for ph, ah in zip(p, apow)]


def _wy_tile(q_scr, k_scr, v_scr, bg, gc, gct, incl, strict, eye_c, expand, fold, n_fused):
    m = bg.shape[0]
    a_c, qk_bd, rhs, qe, gcc = [], [], [], [], []
    for hd in range(GDN_HEADS):
        cs = slice(hd * GDN_DK, (hd + 1) * GDN_DK)
        gcol = GDN_HEADS + hd
        gc_c = gc[:, gcol:gcol + 1]
        gc_r = gct[gcol:gcol + 1, :]
        beta_c = bg[:, hd:hd + 1]
        qh, kh = q_scr[:, cs], k_scr[:, cs]
        dec = jnp.where(incl, jnp.exp(jnp.where(incl, gc_c - gc_r, 0.0)), 0.0)
        kb = kh * beta_c
        gram = _dot_nt(jnp.concatenate([kb, qh], axis=0), kh)
        a_c.append(fold(jnp.where(strict, gram[:m] * dec, 0.0)))
        qk_bd.append(gram[m:] * dec)
        egc = jnp.exp(gc_c)
        rhs.append(jnp.concatenate([v_scr[:, cs] * beta_c, kb * egc], axis=1))
        qe.append(qh * egc)
        gcc.append(gc_c)
    t_c = _neumann_inverse(a_c, eye_c, expand, n_fused)
    sol = [_sdot(expand(t), r) for t, r in zip(t_c, rhs)]
    u = [s[:, :GDN_DV] for s in sol]
    w = [s[:, GDN_DV:] for s in sol]
    return u, w, qk_bd, qe, gcc


def _gdn_finish(x, o_scr, gate_scr, og_ref, wout_ref):
    for hd in range(GDN_HEADS):
        cs = slice(hd * GDN_DV, (hd + 1) * GDN_DV)
        oh = o_scr[:, cs]
        on = oh * lax.rsqrt(jnp.mean(oh * oh, axis=-1, keepdims=True) + EPS) * og_ref[...]
        o_scr[:, cs] = on * gate_scr[:, cs]
    return x + _mm(o_scr[...].astype(BF16), wout_ref[...])


def _gdn_prompt_kernel(x_ref, g_ref, wqkv_ref, wc_ref, wab_ref, alog_ref, dt_ref, wg_ref, og_ref, wout_ref,
                       xo_ref, cst_ref, so_ref,
                       cbuf, halo, s_scr, q_scr, k_scr, v_scr, bg_scr, gate_scr, o_scr, *, m):
    hp = SUBLANES
    r = GDN_C

    @pl.when(pl.program_id(1) == 0)
    def _():
        halo[...] = jnp.zeros_like(halo)
        s_scr[...] = jnp.zeros_like(s_scr)

    x = x_ref[...]
    h = _rms(x, g_ref[...]).astype(BF16)
    pre = _mm(h, wqkv_ref[...])
    cbuf[0:hp, :] = halo[...]
    cbuf[hp:hp + m, :] = pre
    conv = wc_ref[GDN_CONV - 1:GDN_CONV, :] * pre
    for k in range(GDN_CONV - 1):
        off = hp - (GDN_CONV - 1 - k)
        conv = conv + wc_ref[k:k + 1, :] * _rows(cbuf, off, m)
    new_halo = cbuf[m:m + hp, :]
    halo[...] = new_halo
    cst_ref[...] = new_halo
    _gdn_project(h, conv, wab_ref, alog_ref, dt_ref, wg_ref, q_scr, k_scr, v_scr, bg_scr, gate_scr)

    nch = m // r
    row = lax.broadcasted_iota(jnp.int32, (m, m), 0)
    col = lax.broadcasted_iota(jnp.int32, (m, m), 1)
    same = (row // r) == (col // r)
    incl = same & (col <= row)
    strict = same & (col < row)
    tri = jnp.where(incl, 1.0, 0.0).astype(BF16)
    er = lax.broadcasted_iota(jnp.int32, (r, m), 0)
    ec = lax.broadcasted_iota(jnp.int32, (r, m), 1)
    eye_c = jnp.where(ec % r == er, 1.0, 0.0).astype(F32)

    def expand(c):
        return jnp.where(same, jnp.concatenate([c] * nch, axis=0), 0.0)

    def fold(a_bd):
        s = a_bd[0:r]
        for c in range(1, nch):
            s = s + a_bd[c * r:(c + 1) * r]
        return s

    bg = bg_scr[...]
    gc = _cumsum_rows(tri, bg)
    u, w, qk_bd, qe, gcc = _wy_tile(q_scr, k_scr, v_scr, bg, gc, gc.T, incl, strict, eye_c,
                                    expand, fold, _n_fused(r))
    heads = range(GDN_HEADS)
    s_cur = [s_scr[hd] for hd in heads]
    for c in range(nch):
        rs = slice(c * r, (c + 1) * r)
        ws = [_dot1(jnp.concatenate([w[hd][rs], qe[hd][rs]], axis=0), s_cur[hd]) for hd in heads]
        v_new = [u[hd][rs] - ws[hd][0:r] for hd in heads]
        for hd in heads:
            cs = slice(hd * GDN_DV, (hd + 1) * GDN_DV)
            o_scr[rs, cs] = ws[hd][r:2 * r] + _dot1(qk_bd[hd][rs, c * r:(c + 1) * r], v_new[hd])
        for hd in heads:
            cs = slice(hd * GDN_DK, (hd + 1) * GDN_DK)
            g_last = gcc[hd][(c + 1) * r - 1:(c + 1) * r, :]
            kd = k_scr[rs, cs] * jnp.exp(g_last - gcc[hd][rs])
            s_cur[hd] = s_cur[hd] * jnp.exp(g_last) + _dot_tn(kd, v_new[hd])
    for hd in heads:
        s_scr[hd] = s_cur[hd]
    xo_ref[...] = _gdn_finish(x, o_scr, gate_scr, og_ref, wout_ref)
    so_ref[...] = s_scr[...]


def _gdn_sample_kernel(x_ref, p_ref, s_ref, g_ref, wqkv_ref, wc_ref, wab_ref, alog_ref, dt_ref, wg_ref,
                       og_ref, wout_ref,
                       xo_ref, pre_ref, so_ref,
                       q_scr, k_scr, v_scr, bg_scr, gate_scr, o_scr):
    r = GDN_R
    t_len = DEC_SEQ
    x = x_ref[...]
    h = _rms(x, g_ref[...]).astype(BF16)
    pre = _mm(h, wqkv_ref[...])
    pre_ref[...] = pre
    hist = p_ref[...]
    step = lax.broadcasted_iota(jnp.int32, (r, GDN_QKV), 0) % t_len
    conv = wc_ref[GDN_CONV - 1:GDN_CONV, :] * pre
    for s in range(1, GDN_CONV):
        shifted = jnp.where(step >= s, pltpu.roll(pre, s, 0), pltpu.roll(hist, r - t_len + s, 0))
        conv = conv + wc_ref[GDN_CONV - 1 - s:GDN_CONV - s, :] * shifted
    _gdn_project(h, conv, wab_ref, alog_ref, dt_ref, wg_ref, q_scr, k_scr, v_scr, bg_scr, gate_scr)

    row = lax.broadcasted_iota(jnp.int32, (r, r), 0)
    col = lax.broadcasted_iota(jnp.int32, (r, r), 1)
    same = (row // t_len) == (col // t_len)
    incl = same & (col <= row)
    strict = same & (col < row)
    eye = jnp.where(col == row, 1.0, 0.0).astype(F32)
    tri = jnp.where(incl, 1.0, 0.0).astype(BF16)
    bg = bg_scr[...]
    gc = _cumsum_rows(tri, bg)
    same_matrix = lambda c: c
    u, w, qk_bd, qe, gcc = _wy_tile(q_scr, k_scr, v_scr, bg, gc, gc.T, incl, strict, eye,
                                    same_matrix, same_matrix, _n_fused(t_len))
    zeros8 = jnp.zeros((t_len, GDN_DK), F32)
    for hd in range(GDN_HEADS):
        cs = slice(hd * GDN_DK, (hd + 1) * GDN_DK)
        v_new, o_inter = [], []
        for b in range(GDN_SB):
            rows = slice(b * t_len, (b + 1) * t_len)
            ws = _dot1(jnp.concatenate([w[hd][rows], qe[hd][rows]], axis=0), s_ref[b, hd])
            v_new.append(u[hd][rows] - ws[0:t_len])
            o_inter.append(ws[t_len:2 * t_len])
        o_scr[:, cs] = jnp.concatenate(o_inter, axis=0) + _dot1(qk_bd[hd], jnp.concatenate(v_new, axis=0))
        for b in range(GDN_SB):
            rows = slice(b * t_len, (b + 1) * t_len)
            g_last = gcc[hd][(b + 1) * t_len - 1:(b + 1) * t_len, :]
            kd = k_scr[rows, cs] * jnp.exp(g_last - gcc[hd][rows])
            kd16 = jnp.concatenate([kd, zeros8], axis=0)
            vn16 = jnp.concatenate([v_new[b], zeros8], axis=0)
            so_ref[b, hd] = s_ref[b, hd] * jnp.exp(g_last) + _dot_tn(kd16, vn16)
    xo_ref[...] = _gdn_finish(x, o_scr, gate_scr, og_ref, wout_ref)


def _gdn_weight_specs(wqkv, wc, wab, wg, wout):
    d = D_MODEL
    return [_resident((1, d)), _resident(wqkv.shape), _resident(wc.shape), _resident(wab.shape),
            _resident((1, AB_PAD)), _resident((1, AB_PAD)), _resident(wg.shape),
            _resident((1, GDN_DV)), _resident(wout.shape)]


def _gdn_prompt_call(x, g, wqkv, wc, wab, alog, dt, wg, og, wout):
    bsz, t, d = x.shape
    m = GDN_PROMPT_M
    hv = GDN_HEADS * GDN_DV
    grid = (bsz, t // m)
    s_spec = pl.BlockSpec((None, GDN_HEADS, GDN_DK, GDN_DV), lambda b, t: (b, 0, 0, 0))
    return pl.pallas_call(
        functools.partial(_gdn_prompt_kernel, m=m), grid=grid,
        in_specs=[_prompt_tile_spec(m, d)] + _gdn_weight_specs(wqkv, wc, wab, wg, wout),
        out_specs=[_prompt_tile_spec(m, d), _prompt_seq_spec(SUBLANES, GDN_QKV), s_spec],
        out_shape=[jax.ShapeDtypeStruct(x.shape, F32),
                   jax.ShapeDtypeStruct((bsz, SUBLANES, GDN_QKV), F32),
                   jax.ShapeDtypeStruct((bsz, GDN_HEADS, GDN_DK, GDN_DV), F32)],
        scratch_shapes=[pltpu.VMEM((SUBLANES + m, GDN_QKV), F32), pltpu.VMEM((SUBLANES, GDN_QKV), F32),
                        pltpu.VMEM((GDN_HEADS, GDN_DK, GDN_DV), F32),
                        pltpu.VMEM((m, hv), F32), pltpu.VMEM((m, hv), F32), pltpu.VMEM((m, hv), F32),
                        pltpu.VMEM((m, AB_PAD), F32), pltpu.VMEM((m, hv), F32), pltpu.VMEM((m, hv), F32)],
        compiler_params=_params(2), name="gdn_prompt")(x, g, wqkv, wc, wab, alog, dt, wg, og, wout)


def _gdn_sample_call(x, p, s0, g, wqkv, wc, wab, alog, dt, wg, og, wout):
    rows, d = x.shape
    r = GDN_R
    hv = GDN_HEADS * GDN_DV
    grid = (rows // r,)
    row_spec = lambda c: pl.BlockSpec((r, c), lambda j: (j, 0))
    s_spec = pl.BlockSpec((GDN_SB, GDN_HEADS, GDN_DK, GDN_DV), lambda j: (j, 0, 0, 0))
    tile = lambda c: pltpu.VMEM((r, c), F32)
    return pl.pallas_call(
        _gdn_sample_kernel, grid=grid,
        in_specs=[row_spec(d), row_spec(GDN_QKV), s_spec] + _gdn_weight_specs(wqkv, wc, wab, wg, wout),
        out_specs=[row_spec(d), row_spec(GDN_QKV), s_spec],
        out_shape=[jax.ShapeDtypeStruct(x.shape, F32), jax.ShapeDtypeStruct((rows, GDN_QKV), F32),
                   jax.ShapeDtypeStruct(s0.shape, F32)],
        scratch_shapes=[tile(hv), tile(hv), tile(hv), tile(AB_PAD), tile(hv), tile(hv)],
        compiler_params=_params(1), name="gdn_sample")(x, p, s0, g, wqkv, wc, wab, alog, dt, wg, og, wout)


def _row(v):
    return v.reshape(1, -1)


def kernel(x_prompt, x_sample, state_pool, state_conv, state_gdn_conv, state_gdn_S, state_ffn_conv,
           norm_mix_g, norm_ffn_g, norm_final_g,
           a_w_in, a_b_in, a_ln_g, a_ln_b, a_w_s, a_b_s, a_w_out,
           b_w_grp, b_scale,
           c_w_pw1, c_b_pw1, c_w_dw, c_b_dw, c_ln_g, c_ln_b, c_w_pw2, c_b_pw2,
           d_w_qkv, d_w_conv, d_w_ab, d_dt_bias, d_a_log, d_w_g, d_o_norm_g, d_w_out,
           f_w_up, f_w_dw, f_b_dw, f_w_down):
    assert DEPTH == 4 and x_prompt.shape == (BATCH, SEQ, D_MODEL) and x_sample.shape == (DEC_BATCH, DEC_SEQ, D_MODEL)
    d = D_MODEL
    xp = x_prompt
    xs = x_sample
    wup_all = f_w_up.astype(BF16)
    wdown_all = f_w_down.astype(BF16)
    bdw_all = f_b_dw.reshape(DEPTH, 1, D_FF)

    def ffn(i, xp, xs):
        g = _row(norm_ffn_g[i])
        fg = _row(norm_final_g) if i == DEPTH - 1 else None
        outs_p = _ffn_call(xp, None, g, wup_all, f_w_dw, bdw_all, wdown_all, fg, layer=i, sample=False)
        outs_s = _ffn_call(xs, state_ffn_conv[i], g, wup_all, f_w_dw, bdw_all, wdown_all, fg,
                           layer=i, sample=True)
        return outs_p, outs_s, outs_p[1][:, SUBLANES - FFN_BUF:, :], outs_s[1]

    ffn_p, ffn_s = [], []

    g = _row(norm_mix_g[0])
    win, bin_ = a_w_in[0].astype(BF16), _row(a_b_in[0])
    lng, lnb = _row(a_ln_g[0]), _row(a_ln_b[0])
    wout = a_w_out[0].astype(BF16)
    bsx = jnp.repeat(a_b_s[0].T, A_GDIM, axis=1)
    wsx = jnp.repeat(a_w_s[0][:, :DEC_SEQ, :DEC_SEQ].transpose(1, 2, 0), A_GDIM, axis=2)
    wsx = wsx.reshape(DEC_SEQ * DEC_SEQ, d)
    xp, p_chunk_v = _mixa_prompt_call(xp, g, win, bin_, lng, lnb, a_w_s[0], bsx, wout)
    xs, s_chunk_v = _mixa_sample_call(xs, g, win, bin_, lng, lnb, wsx, bsx, wout)
    op, os_, fp, fs = ffn(0, xp, xs)
    xp, xs = op[0], os_[0]
    ffn_p.append(fp); ffn_s.append(fs)

    g = _row(norm_mix_g[1])
    wg, sc = b_w_grp[0].astype(BF16), _row(b_scale[0])
    xp, pool_p = _pool_call(xp, None, g, wg, sc, sample=False)
    xs, s_pool = _pool_call(xs, state_pool[0], g, wg, sc, sample=True)
    p_pool = pool_p[:, _round_up(POOL_BUF, SUBLANES) - POOL_BUF:, :]
    op, os_, fp, fs = ffn(1, xp, xs)
    xp, xs = op[0], os_[0]
    ffn_p.append(fp); ffn_s.append(fs)

    g = _row(norm_mix_g[2])
    cargs = (g, c_w_pw1[0].astype(BF16), _row(c_b_pw1[0]), c_w_dw[0], _row(c_b_dw[0]), _row(c_ln_g[0]),
             _row(c_ln_b[0]), c_w_pw2[0].astype(BF16), _row(c_b_pw2[0]))
    xp, conv_p = _conf_call(xp, None, *cargs, sample=False)
    xs, s_conv = _conf_call(xs, state_conv[0], *cargs, sample=True)
    p_conv = conv_p[:, _round_up(CONV_BUF, SUBLANES) - CONV_BUF:, :]
    op, os_, fp, fs = ffn(2, xp, xs)
    xp, xs = op[0], os_[0]
    ffn_p.append(fp); ffn_s.append(fs)

    g = _row(norm_mix_g[3])
    wab = jnp.pad(d_w_ab[0], ((0, 0), (0, AB_PAD - 2 * GDN_HEADS))).astype(BF16)
    lane_pad = (GDN_HEADS, AB_PAD - 2 * GDN_HEADS)
    alog = _row(jnp.pad(d_a_log[0], lane_pad))
    dtb = _row(jnp.pad(d_dt_bias[0], lane_pad))
    dargs = (g, d_w_qkv[0].astype(BF16), d_w_conv[0], wab, alog, dtb, d_w_g[0].astype(BF16),
             _row(d_o_norm_g[0]), d_w_out[0].astype(BF16))
    xp, gconv_p, p_gdn_s = _gdn_prompt_call(xp, *dargs)
    xs_bm = xs.reshape(DEC_BATCH * DEC_SEQ, d)
    hist = jnp.pad(state_gdn_conv[0], ((0, 0), (DEC_SEQ - GDN_BUF, 0), (0, 0))).reshape(DEC_BATCH * DEC_SEQ, GDN_QKV)
    xs_bm, pre_s, s_gdn_s = _gdn_sample_call(xs_bm, hist, state_gdn_S[0], *dargs)
    xs = xs_bm.reshape(DEC_BATCH, DEC_SEQ, d)
    p_gdn_conv = gconv_p[:, SUBLANES - GDN_BUF:, :]
    s_gdn_conv = pre_s.reshape(DEC_BATCH, DEC_SEQ, GDN_QKV)[:, DEC_SEQ - GDN_BUF:, :]
    op, os_, fp, fs = ffn(3, xp, xs)
    ffn_p.append(fp); ffn_s.append(fs)
    y_prompt = op[2]
    y_sample = os_[2]

    return (y_prompt, y_sample, p_chunk_v[None], s_chunk_v[None], p_pool[None], s_pool[None],
            p_conv[None], s_conv[None], p_gdn_conv[None], s_gdn_conv[None], p_gdn_s[None], s_gdn_s[None],
            jnp.stack(ffn_p), jnp.stack(ffn_s))
```

```python
import functools

import numpy as np
import jax
import jax.numpy as jnp
from jax import lax
from jax.experimental import pallas as pl
from jax.experimental.pallas import tpu as pltpu

F32 = jnp.float32
BF16 = jnp.bfloat16

D_MODEL = 1024
BATCH = 8
SEQ = 2048
DEPTH = 4
DEC_BATCH = 128
DEC_SEQ = 8
PAST_LEN = 16384
EPS = 1e-6
CHUNK = 128
A_GROUPS = 8
A_GDIM = D_MODEL // A_GROUPS
POOL_WINDOWS = (2, 4, 8, 16)
POOL_GDIM = D_MODEL // len(POOL_WINDOWS)
POOL_BUF = max(POOL_WINDOWS) - 1
CONV_WIDTH = 31
CONV_BUF = CONV_WIDTH - 1
GDN_HEADS = 8
GDN_DK = 128
GDN_DV = 128
GDN_CONV = 4
GDN_BUF = GDN_CONV - 1
GDN_QKV = GDN_HEADS * (2 * GDN_DK + GDN_DV)
D_FF = 2816
FFN_CONV = 3
FFN_BUF = FFN_CONV - 1

SUBLANES = 8
LANES = 128
VMEM_LIMIT_BYTES = 60 * 1024 * 1024

FFN_FC = 256
FFN_NFC = D_FF // FFN_FC
SAMPLE_BB = 32
SAMPLE_M = DEC_SEQ * SAMPLE_BB
SAMPLE_NB = DEC_BATCH // SAMPLE_BB
FFN_PROMPT_M = 1024
MIXA_PROMPT_M = 1024
POOL_PROMPT_M = 512
CONF_PROMPT_M = 512
GDN_C = 64
GDN_PACK_M = 256
GDN_PROMPT_M = 512
GDN_R = 128
GDN_SB = GDN_R // DEC_SEQ
AB_PAD = LANES

assert PAST_LEN % CHUNK == 0 and DEC_SEQ <= CHUNK and SEQ % CHUNK == 0 and SEQ % GDN_PROMPT_M == 0


def _n_fused(nilpotency):
    levels = nilpotency.bit_length() - 1
    assert 1 << levels == nilpotency and levels >= 2
    return levels - 2


def _round_up(n, m):
    return (n + m - 1) // m * m


def _rms(x, g):
    return x * lax.rsqrt(jnp.mean(x * x, axis=-1, keepdims=True) + EPS) * g


def _ln(x, g, b):
    mu = jnp.mean(x, axis=-1, keepdims=True)
    xc = x - mu
    var = jnp.mean(xc * xc, axis=-1, keepdims=True)
    return xc * lax.rsqrt(var + EPS) * g + b


def _sigmoid(x):
    return 1.0 / (1.0 + jnp.exp(-x))


def _silu(x):
    return x * _sigmoid(x)


def _gelu_tanh(x):
    c = float(np.sqrt(2.0 / np.pi))
    return x * (0.5 * (1.0 + jnp.tanh(c * (x + 0.044715 * (x * x * x)))))


def _softplus(x):
    return jnp.maximum(x, 0.0) + jnp.log1p(jnp.exp(-jnp.abs(x)))


def _mm(a, b):
    return jnp.dot(a, b, preferred_element_type=F32)


def _dot1(a, b):
    return _mm(a.astype(BF16), b.astype(BF16))


def _dot_nt(a, b):
    return lax.dot_general(a.astype(BF16), b.astype(BF16), (((1,), (1,)), ((), ())),
                           preferred_element_type=F32)


def _dot_tn(a, b):
    return lax.dot_general(a.astype(BF16), b.astype(BF16), (((0,), (0,)), ((), ())),
                           preferred_element_type=F32)


def _split2(a):
    hi = a.astype(BF16)
    lo = (a - hi.astype(F32)).astype(BF16)
    return hi, lo


def _rows(ref, start, rows, cs=slice(None)):
    base = start // SUBLANES * SUBLANES
    delta = start - base
    if delta == 0:
        return ref[start:start + rows, cs]
    end = _round_up(start + rows, SUBLANES)
    return pltpu.roll(ref[base:end, cs], (end - base) - delta, 0)[0:rows]


def _cumsum_rows(tri_bf16, g):
    hi = g.astype(BF16)
    r = g - hi.astype(F32)
    mid = r.astype(BF16)
    lo = (r - mid.astype(F32)).astype(BF16)
    return _mm(tri_bf16, hi) + (_mm(tri_bf16, mid) + _mm(tri_bf16, lo))


def _resident(shape):
    nd = len(shape)
    return pl.BlockSpec(shape, lambda *_: (0,) * nd, pipeline_mode=pl.Buffered(1))


def _params(n_axes):
    return pltpu.CompilerParams(dimension_semantics=("arbitrary",) * n_axes,
                                vmem_limit_bytes=VMEM_LIMIT_BYTES)


def _prompt_tile_spec(m, c):
    return pl.BlockSpec((None, m, c), lambda b, t: (b, t, 0))


def _prompt_seq_spec(r, c):
    return pl.BlockSpec((None, r, c), lambda b, t: (b, 0, 0))


def _sample_spec(t, c):
    return pl.BlockSpec((SAMPLE_BB, t, c), lambda j: (j, 0, 0))


def _load_tm(ref, nt, cs=slice(None)):
    return jnp.concatenate([ref[:, t, cs] for t in range(nt)], axis=0)


def _store_tm(ref, val, nt, cs=slice(None)):
    bb = val.shape[0] // nt
    for t in range(nt):
        ref[:, t, cs] = val[t * bb:(t + 1) * bb]


def _ffn_kernel(*refs, sample, last, m, st, hp):
    it = iter(refs)
    x_ref = next(it)
    st_ref = next(it) if sample else None
    g_ref, wup_ref, wdw_ref, bdw_ref, wdown_ref = (next(it) for _ in range(5))
    fg_ref = next(it) if last else None
    xo_ref = next(it)
    sto_ref = next(it)
    yf_ref = next(it) if last else None
    h_ref, y_ref, abuf = next(it), next(it), next(it)
    halo = None if sample else next(it)

    x = _load_tm(x_ref, DEC_SEQ) if sample else x_ref[...]
    h_ref[...] = _rms(x, g_ref[...]).astype(BF16)
    if not sample:
        @pl.when(pl.program_id(1) == 0)
        def _():
            halo[...] = jnp.zeros_like(halo)

    for f in range(FFN_NFC):
        cs = slice(f * FFN_FC, (f + 1) * FFN_FC)
        vs = slice(D_FF + f * FFN_FC, D_FF + (f + 1) * FFN_FC)
        h = h_ref[...]
        a = _mm(h, wup_ref[:, cs])
        b = _mm(h, wup_ref[:, vs])
        if sample:
            for j in range(FFN_BUF):
                abuf[j * st:(j + 1) * st, :] = st_ref[:, j, cs]
        else:
            abuf[0:hp, :] = halo[:, cs]
        abuf[hp:hp + m, :] = a
        conv = (wdw_ref[0:1, cs] * _rows(abuf, hp - 2 * st, m) + wdw_ref[1:2, cs] * _rows(abuf, hp - st, m)
                + wdw_ref[2:3, cs] * a)
        new_halo = abuf[m:m + hp, :]
        if sample:
            _store_tm(sto_ref, new_halo, FFN_BUF, cs)
        else:
            halo[:, cs] = new_halo
        y_ref[:, cs] = (_silu(conv + bdw_ref[:, cs]) * b).astype(BF16)
    out = x + _mm(y_ref[...], wdown_ref[...])
    if sample:
        _store_tm(xo_ref, out, DEC_SEQ)
    else:
        xo_ref[...] = out
        sto_ref[...] = halo[...]
    if last:
        yf = _rms(out, fg_ref[...])
        if sample:
            _store_tm(yf_ref, yf, DEC_SEQ)
        else:
            yf_ref[...] = yf


def _layer_resident(shape, layer):
    nd = len(shape) - 1
    return pl.BlockSpec((None,) + tuple(shape[1:]), lambda *_: (layer,) + (0,) * nd,
                        pipeline_mode=pl.Buffered(1))


def _ffn_call(x, state, g, wup, wdw, bdw, wdown, final_g, *, layer, sample):
    last = final_g is not None
    d = D_MODEL
    if sample:
        bsz = x.shape[0]
        m, st, hp = SAMPLE_M, SAMPLE_BB, FFN_BUF * SAMPLE_BB
        grid = (bsz // SAMPLE_BB,)
        x_spec = _sample_spec(DEC_SEQ, d)
        st_spec = _sample_spec(FFN_BUF, D_FF)
        st_shape = (bsz, FFN_BUF, D_FF)
    else:
        bsz, t, _ = x.shape
        m, st, hp = FFN_PROMPT_M, 1, SUBLANES
        grid = (bsz, t // m)
        x_spec = _prompt_tile_spec(m, d)
        st_spec = _prompt_seq_spec(hp, D_FF)
        st_shape = (bsz, hp, D_FF)
    in_specs = [x_spec] + ([st_spec] if sample else []) + [
        _resident((1, d)), _layer_resident(wup.shape, layer), _layer_resident(wdw.shape, layer),
        _layer_resident(bdw.shape, layer), _layer_resident(wdown.shape, layer)] + (
        [_resident((1, d))] if last else [])
    args = [x] + ([state] if sample else []) + [g, wup, wdw, bdw, wdown] + ([final_g] if last else [])
    out_specs = [x_spec, st_spec] + ([x_spec] if last else [])
    out_shape = [jax.ShapeDtypeStruct(x.shape, F32), jax.ShapeDtypeStruct(st_shape, F32)] + (
        [jax.ShapeDtypeStruct(x.shape, F32)] if last else [])
    scratch = [pltpu.VMEM((m, d), BF16), pltpu.VMEM((m, D_FF), BF16), pltpu.VMEM((hp + m, FFN_FC), F32)]
    if not sample:
        scratch.append(pltpu.VMEM((hp, D_FF), F32))
    return pl.pallas_call(
        functools.partial(_ffn_kernel, sample=sample, last=last, m=m, st=st, hp=hp),
        grid=grid, in_specs=in_specs, out_specs=out_specs, out_shape=out_shape,
        scratch_shapes=scratch, compiler_params=_params(len(grid)),
        name="ffn_sample" if sample else "ffn_prompt")(*args)


def _mixa_prompt_kernel(x_ref, g_ref, win_ref, bin_ref, lng_ref, lnb_ref, ws_ref, bsx_ref, wout_ref,
                        xo_ref, vo_ref, h_ref, vb_ref, z_ref, *, m):
    d = D_MODEL
    x = x_ref[...]
    h_ref[...] = _rms(x, g_ref[...]).astype(BF16)
    vv = _gelu_tanh(_mm(h_ref[...], win_ref[:, d:2 * d]) + bin_ref[:, d:2 * d])
    v = _ln(vv, lng_ref[...], lnb_ref[...])

    @pl.when(pl.program_id(1) == pl.num_programs(1) - 1)
    def _():
        vo_ref[...] = v[m - CHUNK:m, :]

    vb_ref[...] = v.astype(BF16)
    row = lax.broadcasted_iota(jnp.int32, (CHUNK, CHUNK), 0)
    col = lax.broadcasted_iota(jnp.int32, (CHUNK, CHUNK), 1)
    causal = col <= row
    gpair = 2 * A_GDIM
    for gp in range(A_GROUPS // 2):
        ps = slice(gp * gpair, (gp + 1) * gpair)
        u = _gelu_tanh(_mm(h_ref[...], win_ref[:, ps]) + bin_ref[:, ps])
        for half in range(2):
            grp = 2 * gp + half
            cs = slice(grp * A_GDIM, (grp + 1) * A_GDIM)
            us = slice(half * A_GDIM, (half + 1) * A_GDIM)
            wsg = jnp.where(causal, ws_ref[grp], 0.0).astype(BF16)
            for c in range(m // CHUNK):
                rs = slice(c * CHUNK, (c + 1) * CHUNK)
                mixed = _mm(wsg, vb_ref[rs, cs]) + bsx_ref[:, cs]
                z_ref[rs, cs] = (u[rs, us] * mixed).astype(BF16)
    xo_ref[...] = x + _mm(z_ref[...], wout_ref[...])


def _mixa_sample_kernel(x_ref, g_ref, win_ref, bin_ref, lng_ref, lnb_ref, wsx_ref, bsx_ref, wout_ref,
                        xo_ref, vo_ref, z_ref, *, bb):
    d = D_MODEL
    x = _load_tm(x_ref, DEC_SEQ)
    h = _rms(x, g_ref[...]).astype(BF16)
    u = _gelu_tanh(_mm(h, win_ref[:, 0:d]) + bin_ref[:, 0:d])
    vv = _gelu_tanh(_mm(h, win_ref[:, d:2 * d]) + bin_ref[:, d:2 * d])
    v = _ln(vv, lng_ref[...], lnb_ref[...])
    _store_tm(vo_ref, v, DEC_SEQ)
    for t in range(DEC_SEQ):
        mixed = bsx_ref[t:t + 1, :]
        for j in range(t + 1):
            mixed = mixed + wsx_ref[t * DEC_SEQ + j:t * DEC_SEQ + j + 1, :] * v[j * bb:(j + 1) * bb, :]
        z_ref[t * bb:(t + 1) * bb, :] = (u[t * bb:(t + 1) * bb, :] * mixed).astype(BF16)
    _store_tm(xo_ref, x + _mm(z_ref[...], wout_ref[...]), DEC_SEQ)


def _mixa_prompt_call(x, g, win, bin_, lng, lnb, ws, bsx, wout):
    bsz, t, d = x.shape
    m = MIXA_PROMPT_M
    grid = (bsz, t // m)
    return pl.pallas_call(
        functools.partial(_mixa_prompt_kernel, m=m), grid=grid,
        in_specs=[_prompt_tile_spec(m, d), _resident((1, d)), _resident(win.shape), _resident(bin_.shape),
                  _resident((1, d)), _resident((1, d)), _resident(ws.shape), _resident(bsx.shape),
                  _resident(wout.shape)],
        out_specs=[_prompt_tile_spec(m, d), _prompt_seq_spec(CHUNK, d)],
        out_shape=[jax.ShapeDtypeStruct(x.shape, F32), jax.ShapeDtypeStruct((bsz, CHUNK, d), F32)],
        scratch_shapes=[pltpu.VMEM((m, d), BF16), pltpu.VMEM((m, d), BF16), pltpu.VMEM((m, d), BF16)],
        compiler_params=_params(2), name="mixa_prompt")(x, g, win, bin_, lng, lnb, ws, bsx, wout)


def _mixa_sample_call(x, g, win, bin_, lng, lnb, wsx, bsx, wout):
    bsz, t, d = x.shape
    return pl.pallas_call(
        functools.partial(_mixa_sample_kernel, bb=SAMPLE_BB), grid=(bsz // SAMPLE_BB,),
        in_specs=[_sample_spec(t, d), _resident((1, d)), _resident(win.shape), _resident(bin_.shape),
                  _resident((1, d)), _resident((1, d)), _resident(wsx.shape), _resident(bsx.shape),
                  _resident(wout.shape)],
        out_specs=[_sample_spec(t, d), _sample_spec(t, d)],
        out_shape=[jax.ShapeDtypeStruct(x.shape, F32), jax.ShapeDtypeStruct(x.shape, F32)],
        scratch_shapes=[pltpu.VMEM((SAMPLE_M, d), BF16)],
        compiler_params=_params(1), name="mixa_sample")(x, g, win, bin_, lng, lnb, wsx, bsx, wout)


def _pool_kernel(*refs, sample, m, st, hp):
    it = iter(refs)
    x_ref = next(it)
    st_ref = next(it) if sample else None
    g_ref, wg_ref, sc_ref = next(it), next(it), next(it)
    xo_ref, sto_ref = next(it), next(it)
    hbuf = next(it)
    halo = None if sample else next(it)

    x = _load_tm(x_ref, DEC_SEQ) if sample else x_ref[...]
    h = _rms(x, g_ref[...])
    if sample:
        for j in range(POOL_BUF):
            hbuf[j * st:(j + 1) * st, :] = st_ref[:, j, :]
        pos0 = PAST_LEN
    else:
        @pl.when(pl.program_id(1) == 0)
        def _():
            halo[...] = jnp.zeros_like(halo)
        hbuf[0:hp, :] = halo[...]
        pos0 = pl.program_id(1) * m
    hbuf[hp:hp + m, :] = h
    new_halo = hbuf[m:m + hp, :]
    if sample:
        _store_tm(sto_ref, new_halo, POOL_BUF)
    else:
        sto_ref[...] = new_halo
        halo[...] = new_halo
    pos = pos0 + lax.broadcasted_iota(jnp.int32, (m, POOL_GDIM), 0) // st
    outs = []
    for gi, win in enumerate(POOL_WINDOWS):
        cs = slice(gi * POOL_GDIM, (gi + 1) * POOL_GDIM)
        acc = hbuf[:, cs]
        span = 1
        while span < win:
            acc = acc + pltpu.roll(acc, span * st, 0)
            span *= 2
        s = acc[hp:hp + m]
        cnt = jnp.minimum(pos + 1, win).astype(F32)
        pooled = s / cnt - h[:, cs]
        outs.append(x[:, cs] + _dot1(pooled, wg_ref[gi]) * sc_ref[:, cs])
    if sample:
        for gi, o in enumerate(outs):
            _store_tm(xo_ref, o, DEC_SEQ, slice(gi * POOL_GDIM, (gi + 1) * POOL_GDIM))
    else:
        for gi, o in enumerate(outs):
            xo_ref[:, gi * POOL_GDIM:(gi + 1) * POOL_GDIM] = o


def _pool_call(x, state, g, wg, sc, *, sample):
    d = D_MODEL
    if sample:
        bsz = x.shape[0]
        m, st, hp = SAMPLE_M, SAMPLE_BB, POOL_BUF * SAMPLE_BB
        grid = (bsz // SAMPLE_BB,)
        x_spec, st_spec = _sample_spec(DEC_SEQ, d), _sample_spec(POOL_BUF, d)
        st_shape = (bsz, POOL_BUF, d)
    else:
        bsz, t, _ = x.shape
        m, st, hp = POOL_PROMPT_M, 1, _round_up(POOL_BUF, SUBLANES)
        grid = (bsz, t // m)
        x_spec, st_spec = _prompt_tile_spec(m, d), _prompt_seq_spec(hp, d)
        st_shape = (bsz, hp, d)
    in_specs = [x_spec] + ([st_spec] if sample else []) + [
        _resident((1, d)), _resident(wg.shape), _resident((1, d))]
    args = [x] + ([state] if sample else []) + [g, wg, sc]
    scratch = [pltpu.VMEM((hp + m, d), F32)] + ([] if sample else [pltpu.VMEM((hp, d), F32)])
    return pl.pallas_call(
        functools.partial(_pool_kernel, sample=sample, m=m, st=st, hp=hp), grid=grid,
        in_specs=in_specs, out_specs=[x_spec, st_spec],
        out_shape=[jax.ShapeDtypeStruct(x.shape, F32), jax.ShapeDtypeStruct(st_shape, F32)],
        scratch_shapes=scratch, compiler_params=_params(len(grid)),
        name="pool_sample" if sample else "pool_prompt")(*args)


CONV_RB = 64


def _conf_kernel(*refs, sample, m, st, hp):
    it = iter(refs)
    x_ref = next(it)
    st_ref = next(it) if sample else None
    (g_ref, w1_ref, b1_ref, wdw_ref, bdw_ref, lng_ref, lnb_ref, w2_ref, b2_ref) = (next(it) for _ in range(9))
    xo_ref, sto_ref = next(it), next(it)
    zbuf, cbuf = next(it), next(it)
    halo = None if sample else next(it)
    d = D_MODEL

    x = _load_tm(x_ref, DEC_SEQ) if sample else x_ref[...]
    h = _rms(x, g_ref[...]).astype(BF16)
    a = _mm(h, w1_ref[:, 0:d]) + b1_ref[:, 0:d]
    gt = _mm(h, w1_ref[:, d:2 * d]) + b1_ref[:, d:2 * d]
    z = a * _sigmoid(gt)
    if sample:
        for j in range(CONV_BUF):
            zbuf[j * st:(j + 1) * st, :] = st_ref[:, j, :]
    else:
        @pl.when(pl.program_id(1) == 0)
        def _():
            halo[...] = jnp.zeros_like(halo)
        zbuf[0:hp, :] = halo[...]
    zbuf[hp:hp + m, :] = z
    new_halo = zbuf[m:m + hp, :]
    if sample:
        _store_tm(sto_ref, new_halo, CONV_BUF)
    else:
        sto_ref[...] = new_halo
        halo[...] = new_halo
    for c0 in range(0, d, LANES):
        cs = slice(c0, c0 + LANES)
        for r0 in range(0, m, CONV_RB):
            first = hp - (CONV_WIDTH - 1) * st + r0
            acc = None
            if st % SUBLANES == 0:
                for k in range(CONV_WIDTH):
                    term = wdw_ref[k:k + 1, cs] * zbuf[first + k * st:first + k * st + CONV_RB, cs]
                    acc = term if acc is None else acc + term
            else:
                assert st == 1
                for s in range(SUBLANES):
                    taps = range(s, CONV_WIDTH, SUBLANES)
                    win = _rows(zbuf, first + s, CONV_RB + (len(taps) - 1) * SUBLANES, cs)
                    for q, k in enumerate(taps):
                        term = wdw_ref[k:k + 1, cs] * win[q * SUBLANES:q * SUBLANES + CONV_RB]
                        acc = term if acc is None else acc + term
            cbuf[r0:r0 + CONV_RB, cs] = acc
    c = _silu(_ln(cbuf[...] + bdw_ref[...], lng_ref[...], lnb_ref[...]))
    out = x + _mm(c.astype(BF16), w2_ref[...]) + b2_ref[...]
    if sample:
        _store_tm(xo_ref, out, DEC_SEQ)
    else:
        xo_ref[...] = out


def _conf_call(x, state, g, w1, b1, wdw, bdw, lng, lnb, w2, b2, *, sample):
    d = D_MODEL
    if sample:
        bsz = x.shape[0]
        m, st, hp = SAMPLE_M, SAMPLE_BB, CONV_BUF * SAMPLE_BB
        grid = (bsz // SAMPLE_BB,)
        x_spec, st_spec = _sample_spec(DEC_SEQ, d), _sample_spec(CONV_BUF, d)
        st_shape = (bsz, CONV_BUF, d)
    else:
        bsz, t, _ = x.shape
        m, st, hp = CONF_PROMPT_M, 1, _round_up(CONV_BUF, SUBLANES)
        grid = (bsz, t // m)
        x_spec, st_spec = _prompt_tile_spec(m, d), _prompt_seq_spec(hp, d)
        st_shape = (bsz, hp, d)
    in_specs = [x_spec] + ([st_spec] if sample else []) + [
        _resident((1, d)), _resident(w1.shape), _resident(b1.shape), _resident(wdw.shape),
        _resident((1, d)), _resident((1, d)), _resident((1, d)), _resident(w2.shape), _resident((1, d))]
    args = [x] + ([state] if sample else []) + [g, w1, b1, wdw, bdw, lng, lnb, w2, b2]
    scratch = [pltpu.VMEM((hp + m, d), F32), pltpu.VMEM((m, d), F32)] + (
        [] if sample else [pltpu.VMEM((hp, d), F32)])
    return pl.pallas_call(
        functools.partial(_conf_kernel, sample=sample, m=m, st=st, hp=hp), grid=grid,
        in_specs=in_specs, out_specs=[x_spec, st_spec],
        out_shape=[jax.ShapeDtypeStruct(x.shape, F32), jax.ShapeDtypeStruct(st_shape, F32)],
        scratch_shapes=scratch, compiler_params=_params(len(grid)),
        name="conf_sample" if sample else "conf_prompt")(*args)


GDN_HK = GDN_HEADS * GDN_DK


def _gdn_qkv_part(part, conv, q_scr, k_scr, v_scr):
    act = _silu(conv)
    if part == 2:
        v_scr[...] = act
        return
    dst, scale = (q_scr, GDN_DK ** -0.5) if part == 0 else (k_scr, 1.0)
    for hd in range(GDN_HEADS):
        cs = slice(hd * GDN_DK, (hd + 1) * GDN_DK)
        ah = act[:, cs]
        nrm = ah * lax.rsqrt(jnp.sum(ah * ah, axis=-1, keepdims=True) + EPS)
        dst[:, cs] = nrm * scale if part == 0 else nrm


def _gdn_gates(h, wab_ref, alog_ref, dt_ref, wg_ref, bg_scr, gate_scr):
    ab = _mm(h, wab_ref[...])
    lane = lax.broadcasted_iota(jnp.int32, ab.shape, 1)
    beta = _sigmoid(ab)
    gl = -jnp.exp(alog_ref[...]) * _softplus(ab + dt_ref[...])
    bg_scr[...] = jnp.where(lane < GDN_HEADS, beta, gl)
    gate_scr[...] = _silu(_mm(h, wg_ref[...]))


def _sdot(lhs, rhs):
    return _dot1(lhs, rhs)


def _neumann_inverse(a_list, eye_c, expand, n_fused):
    r = a_list[0].shape[0]
    apow = [_sdot(a, expand(a)) for a in a_list]
    p = [eye_c - a for a in a_list]
    for _ in range(n_fused):
        outs = [_sdot(jnp.concatenate([ph, ah], axis=0), expand(ah)) for ph, ah in zip(p, apow)]
        p = [ph + o[:r] for ph, o in zip(p, outs)]
        apow = [o[r:] for o in outs]
    return [ph + _sdot(ph, expand(ah)) for ph, ah in zip(p, apow)]


def _wy_tile(q_scr, k_scr, v_scr, rows, bg, gc, gct, incl, strict, eye_c, expand, fold, n_fused):
    m = bg.shape[0]
    a_c, qk_bd, rhs, qe, gcc = [], [], [], [], []
    for hd in range(GDN_HEADS):
        cs = slice(hd * GDN_DK, (hd + 1) * GDN_DK)
        gcol = GDN_HEADS + hd
        gc_c = gc[:, gcol:gcol + 1]
        gc_r = gct[gcol:gcol + 1, :]
        beta_c = bg[:, hd:hd + 1]
        qh, kh = q_scr[rows, cs], k_scr[rows, cs]
        dec =jnp.where(incl, jnp.exp(jnp.where(incl, gc_c - gc_r, 0.0)), 0.0)
        kb = kh * beta_c
        gram = _dot_nt(jnp.concatenate([kb, qh], axis=0), kh)
        a_c.append(fold(jnp.where(strict, gram[:m] * dec, 0.0)))
        qk_bd.append(gram[m:] * dec)
        egc = jnp.exp(gc_c)
        rhs.append(jnp.concatenate([v_scr[rows, cs] * beta_c, kb * egc], axis=1))
        qe.append(qh * egc)
        gcc.append(gc_c)
    t_c = _neumann_inverse(a_c, eye_c, expand, n_fused)
    sol = [_sdot(expand(t), r) for t, r in zip(t_c, rhs)]
    u = [s[:, :GDN_DV] for s in sol]
    w = [s[:, GDN_DV:] for s in sol]
    return u, w, qk_bd, qe, gcc


def _gdn_finish(x, o_scr, gate_scr, og_ref, wout_ref):
    for hd in range(GDN_HEADS):
        cs = slice(hd * GDN_DV, (hd + 1) * GDN_DV)
        oh = o_scr[:, cs]
        on = oh * lax.rsqrt(jnp.mean(oh * oh, axis=-1, keepdims=True) + EPS) * og_ref[...]
        o_scr[:, cs] = on * gate_scr[:, cs]
    return x + _mm(o_scr[...].astype(BF16), wout_ref[...])


def _gdn_prompt_kernel(x_ref, g_ref, wqkv_ref, wc_ref, wab_ref, alog_ref, dt_ref, wg_ref, og_ref, wout_ref,
                       xo_ref, cst_ref, so_ref,
                       cbuf, halo, s_scr, q_scr, k_scr, v_scr, bg_scr, gate_scr, o_scr, *, m):
    hp = SUBLANES
    r = GDN_C

    @pl.when(pl.program_id(1) == 0)
    def _():
        halo[...] = jnp.zeros_like(halo)
        s_scr[...] = jnp.zeros_like(s_scr)

    x = x_ref[...]
    h = _rms(x, g_ref[...]).astype(BF16)
    for part in range(3):
        ps = slice(part * GDN_HK, (part + 1) * GDN_HK)
        pre = _mm(h, wqkv_ref[:, ps])
        cbuf[0:hp, :] = halo[:, ps]
        cbuf[hp:hp + m, :] = pre
        conv = wc_ref[GDN_CONV - 1:GDN_CONV, ps] * pre
        for k in range(GDN_CONV - 1):
            off = hp - (GDN_CONV - 1 - k)
            conv = conv + wc_ref[k:k + 1, ps] * _rows(cbuf, off, m)
        new_halo = cbuf[m:m + hp, :]
        halo[:, ps] = new_halo
        cst_ref[:, ps] = new_halo
        _gdn_qkv_part(part, conv, q_scr, k_scr, v_scr)
    _gdn_gates(h, wab_ref, alog_ref, dt_ref, wg_ref, bg_scr, gate_scr)

    pm = GDN_PACK_M
    nch = pm // r
    row = lax.broadcasted_iota(jnp.int32, (pm, pm), 0)
    col = lax.broadcasted_iota(jnp.int32, (pm, pm), 1)
    same = (row // r) == (col // r)
    incl = same & (col <= row)
    strict = same & (col < row)
    tri = jnp.where(incl, 1.0, 0.0).astype(BF16)
    er = lax.broadcasted_iota(jnp.int32, (r, pm), 0)
    ec = lax.broadcasted_iota(jnp.int32, (r, pm), 1)
    eye_c = jnp.where(ec % r == er, 1.0, 0.0).astype(F32)

    def expand(c):
        return jnp.where(same, jnp.concatenate([c] * nch, axis=0), 0.0)

    def fold(a_bd):
        s = a_bd[0:r]
        for c in range(1, nch):
            s = s + a_bd[c * r:(c + 1) * r]
        return s

    heads = range(GDN_HEADS)
    packs = []
    for pk in range(m // pm):
        prs = slice(pk * pm, (pk + 1) * pm)
        bg = bg_scr[prs, :]
        gc = _cumsum_rows(tri, bg)
        packs.append(_wy_tile(q_scr, k_scr, v_scr, prs, bg, gc, gc.T, incl, strict, eye_c,
                              expand, fold, _n_fused(r)))
    s_cur = [s_scr[hd] for hd in heads]
    for pk, (u, w, qk_bd, qe, gcc) in enumerate(packs):
        for c in range(nch):
            rs = slice(c * r, (c + 1) * r)
            ts = slice(pk * pm + c * r, pk * pm + (c + 1) * r)
            ws = [_dot1(jnp.concatenate([w[hd][rs], qe[hd][rs]], axis=0), s_cur[hd]) for hd in heads]
            v_new = [u[hd][rs] - ws[hd][0:r] for hd in heads]
            for hd in heads:
                cs = slice(hd * GDN_DV, (hd + 1) * GDN_DV)
                o_scr[ts, cs] = ws[hd][r:2 * r] + _dot1(qk_bd[hd][rs, c * r:(c + 1) * r], v_new[hd])
            for hd in heads:
                cs = slice(hd * GDN_DK, (hd + 1) * GDN_DK)
                g_last = gcc[hd][(c + 1) * r - 1:(c + 1) * r, :]
                kd = k_scr[ts, cs] * jnp.exp(g_last - gcc[hd][rs])
                s_cur[hd] = s_cur[hd] * jnp.exp(g_last) + _dot_tn(kd, v_new[hd])
    for hd in heads:
        s_scr[hd] = s_cur[hd]
    xo_ref[...] = _gdn_finish(x, o_scr, gate_scr, og_ref, wout_ref)
    so_ref[...] = s_scr[...]


def _gdn_sample_kernel(x_ref, p_ref, s_ref, g_ref, wqkv_ref, wc_ref, wab_ref, alog_ref, dt_ref, wg_ref,
                       og_ref, wout_ref,
                       xo_ref, pre_ref, so_ref,
                       q_scr, k_scr, v_scr, bg_scr, gate_scr, o_scr):
    r = GDN_R
    t_len = DEC_SEQ
    x = x_ref[...]
    h = _rms(x, g_ref[...]).astype(BF16)
    pre = _mm(h, wqkv_ref[...])
    pre_ref[...] = pre
    hist = p_ref[...]
    step = lax.broadcasted_iota(jnp.int32, (r, GDN_QKV), 0) % t_len
    conv = wc_ref[GDN_CONV - 1:GDN_CONV, :] * pre
    for s in range(1, GDN_CONV):
        shifted = jnp.where(step >= s, pltpu.roll(pre, s, 0), pltpu.roll(hist, r - t_len + s, 0))
        conv = conv + wc_ref[GDN_CONV - 1 - s:GDN_CONV - s, :] * shifted
    for part in range(3):
        _gdn_qkv_part(part, conv[:, part * GDN_HK:(part + 1) * GDN_HK], q_scr, k_scr, v_scr)
    _gdn_gates(h, wab_ref, alog_ref, dt_ref, wg_ref, bg_scr, gate_scr)

    row = lax.broadcasted_iota(jnp.int32, (r, r), 0)
    col = lax.broadcasted_iota(jnp.int32, (r, r), 1)
    same = (row // t_len) == (col // t_len)
    incl = same & (col <= row)
    strict = same & (col < row)
    eye = jnp.where(col == row, 1.0, 0.0).astype(F32)
    tri = jnp.where(incl, 1.0, 0.0).astype(BF16)
    bg = bg_scr[...]
    gc = _cumsum_rows(tri, bg)
    same_matrix = lambda c: c
    u, w, qk_bd, qe, gcc = _wy_tile(q_scr, k_scr, v_scr, slice(0, r), bg, gc, gc.T, incl, strict, eye,
                                    same_matrix, same_matrix, _n_fused(t_len))
    zeros8 = jnp.zeros((t_len, GDN_DK), F32)
    for hd in range(GDN_HEADS):
        cs = slice(hd * GDN_DK, (hd + 1) * GDN_DK)
        v_new, o_inter = [], []
        for b in range(GDN_SB):
            rows = slice(b * t_len, (b + 1) * t_len)
            ws = _dot1(jnp.concatenate([w[hd][rows], qe[hd][rows]], axis=0), s_ref[b, hd])
            v_new.append(u[hd][rows] - ws[0:t_len])
            o_inter.append(ws[t_len:2 * t_len])
        o_scr[:, cs] = jnp.concatenate(o_inter, axis=0) + _dot1(qk_bd[hd], jnp.concatenate(v_new, axis=0))
        for b in range(GDN_SB):
            rows = slice(b * t_len, (b + 1) * t_len)
            g_last = gcc[hd][(b + 1) * t_len - 1:(b + 1) * t_len, :]
            kd = k_scr[rows, cs] * jnp.exp(g_last - gcc[hd][rows])
            kd16 = jnp.concatenate([kd, zeros8], axis=0)
            vn16 = jnp.concatenate([v_new[b], zeros8], axis=0)
            so_ref[b, hd] = s_ref[b, hd] * jnp.exp(g_last) + _dot_tn(kd16, vn16)
    xo_ref[...] = _gdn_finish(x, o_scr, gate_scr, og_ref, wout_ref)


def _gdn_weight_specs(wqkv, wc, wab, wg, wout):
    d = D_MODEL
    return [_resident((1, d)), _resident(wqkv.shape), _resident(wc.shape), _resident(wab.shape),
            _resident((1, AB_PAD)), _resident((1, AB_PAD)), _resident(wg.shape),
            _resident((1, GDN_DV)), _resident(wout.shape)]


def _gdn_prompt_call(x, g, wqkv, wc, wab, alog, dt, wg, og, wout):
    bsz, t, d = x.shape
    m = GDN_PROMPT_M
    hv = GDN_HEADS * GDN_DV
    grid = (bsz, t // m)
    s_spec = pl.BlockSpec((None, GDN_HEADS, GDN_DK, GDN_DV), lambda b, t: (b, 0, 0, 0))
    return pl.pallas_call(
        functools.partial(_gdn_prompt_kernel, m=m), grid=grid,
        in_specs=[_prompt_tile_spec(m, d)] + _gdn_weight_specs(wqkv, wc, wab, wg, wout),
        out_specs=[_prompt_tile_spec(m, d), _prompt_seq_spec(SUBLANES, GDN_QKV), s_spec],
        out_shape=[jax.ShapeDtypeStruct(x.shape, F32),
                   jax.ShapeDtypeStruct((bsz, SUBLANES, GDN_QKV), F32),
                   jax.ShapeDtypeStruct((bsz, GDN_HEADS, GDN_DK, GDN_DV), F32)],
        scratch_shapes=[pltpu.VMEM((SUBLANES + m, GDN_HK), F32), pltpu.VMEM((SUBLANES, GDN_QKV), F32),
                        pltpu.VMEM((GDN_HEADS, GDN_DK, GDN_DV), F32),
                        pltpu.VMEM((m, hv), F32), pltpu.VMEM((m, hv), F32), pltpu.VMEM((m, hv), F32),
                        pltpu.VMEM((m, AB_PAD), F32), pltpu.VMEM((m, hv), F32), pltpu.VMEM((m, hv), F32)],
        compiler_params=_params(2), name="gdn_prompt")(x, g, wqkv, wc, wab, alog, dt, wg, og, wout)


def _gdn_sample_call(x, p, s0, g, wqkv, wc, wab, alog, dt, wg, og, wout):
    rows, d = x.shape
    r = GDN_R
    hv = GDN_HEADS * GDN_DV
    grid = (rows // r,)
    row_spec = lambda c: pl.BlockSpec((r, c), lambda j: (j, 0))
    s_spec = pl.BlockSpec((GDN_SB, GDN_HEADS, GDN_DK, GDN_DV), lambda j: (j, 0, 0, 0))
    tile = lambda c: pltpu.VMEM((r, c), F32)
    return pl.pallas_call(
        _gdn_sample_kernel, grid=grid,
        in_specs=[row_spec(d), row_spec(GDN_QKV), s_spec] + _gdn_weight_specs(wqkv, wc, wab, wg, wout),
        out_specs=[row_spec(d), row_spec(GDN_QKV), s_spec],
        out_shape=[jax.ShapeDtypeStruct(x.shape, F32), jax.ShapeDtypeStruct((rows, GDN_QKV), F32),
                   jax.ShapeDtypeStruct(s0.shape, F32)],
        scratch_shapes=[tile(hv), tile(hv), tile(hv), tile(AB_PAD), tile(hv), tile(hv)],
        compiler_params=_params(1), name="gdn_sample")(x, p, s0, g, wqkv, wc, wab, alog, dt, wg, og, wout)


def _row(v):
    return v.reshape(1, -1)


def kernel(x_prompt, x_sample, state_pool, state_conv, state_gdn_conv, state_gdn_S, state_ffn_conv,
           norm_mix_g, norm_ffn_g, norm_final_g,
           a_w_in, a_b_in, a_ln_g, a_ln_b, a_w_s, a_b_s, a_w_out,
           b_w_grp, b_scale,
           c_w_pw1, c_b_pw1, c_w_dw, c_b_dw, c_ln_g, c_ln_b, c_w_pw2, c_b_pw2,
           d_w_qkv, d_w_conv, d_w_ab, d_dt_bias, d_a_log, d_w_g, d_o_norm_g, d_w_out,
           f_w_up, f_w_dw, f_b_dw, f_w_down):
    assert DEPTH == 4 and x_prompt.shape == (BATCH, SEQ, D_MODEL) and x_sample.shape == (DEC_BATCH, DEC_SEQ, D_MODEL)
    d = D_MODEL
    xp = x_prompt
    xs = x_sample
    wup_all = f_w_up.astype(BF16)
    wdown_all = f_w_down.astype(BF16)
    bdw_all = f_b_dw.reshape(DEPTH, 1, D_FF)

    def ffn(i, xp, xs):
        g = _row(norm_ffn_g[i])
        fg = _row(norm_final_g) if i == DEPTH - 1 else None
        outs_p = _ffn_call(xp, None, g, wup_all, f_w_dw, bdw_all, wdown_all, fg, layer=i, sample=False)
        outs_s = _ffn_call(xs, state_ffn_conv[i], g, wup_all, f_w_dw, bdw_all, wdown_all, fg,
                           layer=i, sample=True)
        return outs_p, outs_s, outs_p[1][:, SUBLANES - FFN_BUF:, :], outs_s[1]

    ffn_p, ffn_s = [], []

    g = _row(norm_mix_g[0])
    win, bin_ = a_w_in[0].astype(BF16), _row(a_b_in[0])
    lng, lnb = _row(a_ln_g[0]), _row(a_ln_b[0])
    wout = a_w_out[0].astype(BF16)
    bsx = jnp.repeat(a_b_s[0].T, A_GDIM, axis=1)
    wsx = jnp.repeat(a_w_s[0][:, :DEC_SEQ, :DEC_SEQ].transpose(1, 2, 0), A_GDIM, axis=2)
    wsx = wsx.reshape(DEC_SEQ * DEC_SEQ, d)
    xp, p_chunk_v = _mixa_prompt_call(xp, g, win, bin_, lng, lnb, a_w_s[0], bsx, wout)
    xs, s_chunk_v = _mixa_sample_call(xs, g, win, bin_, lng, lnb, wsx, bsx, wout)
    op, os_, fp, fs = ffn(0, xp, xs)
    xp, xs = op[0], os_[0]
    ffn_p.append(fp); ffn_s.append(fs)

    g = _row(norm_mix_g[1])
    wg, sc = b_w_grp[0].astype(BF16), _row(b_scale[0])
    xp, pool_p = _pool_call(xp, None, g, wg, sc, sample=False)
    xs, s_pool = _pool_call(xs, state_pool[0], g, wg, sc, sample=True)
    p_pool = pool_p[:, _round_up(POOL_BUF, SUBLANES) - POOL_BUF:, :]
    op, os_, fp, fs = ffn(1, xp, xs)
    xp, xs = op[0], os_[0]
    ffn_p.append(fp); ffn_s.append(fs)

    g = _row(norm_mix_g[2])
    cargs = (g, c_w_pw1[0].astype(BF16), _row(c_b_pw1[0]), c_w_dw[0], _row(c_b_dw[0]), _row(c_ln_g[0]),
             _row(c_ln_b[0]), c_w_pw2[0].astype(BF16), _row(c_b_pw2[0]))
    xp, conv_p = _conf_call(xp, None, *cargs, sample=False)
    xs, s_conv = _conf_call(xs, state_conv[0], *cargs, sample=True)
    p_conv = conv_p[:, _round_up(CONV_BUF, SUBLANES) - CONV_BUF:, :]
    op, os_, fp, fs = ffn(2, xp, xs)
    xp, xs = op[0], os_[0]
    ffn_p.append(fp); ffn_s.append(fs)

    g = _row(norm_mix_g[3])
    wab = jnp.pad(d_w_ab[0], ((0, 0), (0, AB_PAD - 2 * GDN_HEADS))).astype(BF16)
    lane_pad = (GDN_HEADS, AB_PAD - 2 * GDN_HEADS)
    alog = _row(jnp.pad(d_a_log[0], lane_pad))
    dtb = _row(jnp.pad(d_dt_bias[0], lane_pad))
    dargs = (g, d_w_qkv[0].astype(BF16), d_w_conv[0], wab, alog, dtb, d_w_g[0].astype(BF16),
             _row(d_o_norm_g[0]), d_w_out[0].astype(BF16))
    xp, gconv_p, p_gdn_s = _gdn_prompt_call(xp, *dargs)
    xs_bm = xs.reshape(DEC_BATCH * DEC_SEQ, d)
    hist = jnp.pad(state_gdn_conv[0], ((0, 0), (DEC_SEQ - GDN_BUF, 0), (0, 0))).reshape(DEC_BATCH * DEC_SEQ, GDN_QKV)
    xs_bm, pre_s, s_gdn_s = _gdn_sample_call(xs_bm, hist, state_gdn_S[0], *dargs)
    xs = xs_bm.reshape(DEC_BATCH, DEC_SEQ, d)
    p_gdn_conv = gconv_p[:, SUBLANES - GDN_BUF:, :]
    s_gdn_conv = pre_s.reshape(DEC_BATCH, DEC_SEQ, GDN_QKV)[:, DEC_SEQ - GDN_BUF:, :]
    op, os_, fp, fs = ffn(3, xp, xs)
    ffn_p.append(fp); ffn_s.append(fs)
    y_prompt = op[2]
    y_sample = os_[2]

    return (y_prompt, y_sample, p_chunk_v[None], s_chunk_v[None], p_pool[None], s_pool[None],
            p_conv[None], s_conv[None], p_gdn_conv[None], s_gdn_conv[None], p_gdn_s[None], s_gdn_s[None],
            jnp.stack(ffn_p), jnp.stack(ffn_s))
```
